```python
import math
import jax, jax.numpy as jnp
from jax import lax
import numpy as np

D_MODEL = 1024
BATCH = 8
SEQ = 2048
DEPTH = 2
DEC_BATCH = 128
DEC_SEQ = 8
PAST_LEN = 2048
PAGE_SIZE = 128

N_MIXERS = 4
GROUP_WIDTH = D_MODEL // N_MIXERS
GROUP_HEADS = 4
HEAD_DIM = GROUP_WIDTH // GROUP_HEADS
HG_DK = HEAD_DIM
HG_DV = HEAD_DIM
GLA_DK = HEAD_DIM // 2
GLA_DV = HEAD_DIM
GLA_GATE_RANK = 16
GLA_GATE_NORM = 16.0
GLA_CHUNK = 32
RW_DECAY_RANK = 32
RW_A_RANK = 32
RW_GATE_RANK = 64
RW_PROJ = 3 * GROUP_WIDTH + RW_DECAY_RANK + RW_A_RANK + RW_GATE_RANK
RW_GN_EPS = 64e-5
SB_BLOCK = 128
SB_BIAS_INIT = -6.0
D_FF = 4 * D_MODEL
RMS_EPS = 1e-6
IN_WIDTHS = (GROUP_WIDTH, GROUP_WIDTH, GROUP_WIDTH, GROUP_WIDTH,
             RW_PROJ,
             GROUP_WIDTH, GROUP_WIDTH, GROUP_WIDTH,
             GROUP_HEADS * GLA_DK, GROUP_HEADS * GLA_DK, GROUP_WIDTH, GLA_GATE_RANK, GROUP_WIDTH)
IN_COLS = sum(IN_WIDTHS)
F32 = jnp.float32

kernel_name = 'hymba_style_hgrn2_rwkv7_stickbreak_gla_step'


def _rms(x, g):
    xf = x.astype(F32)
    y = xf * lax.rsqrt(jnp.mean(xf * xf, axis=-1, keepdims=True) + RMS_EPS)
    return (y * g.astype(F32)).astype(x.dtype)


def _split(a, widths):
    return jnp.split(a, np.cumsum(widths)[:-1].tolist(), axis=-1)


def _chunked_gla(q, k, v, log_g, s0):
    q, k, v, log_g = (t.astype(F32) for t in (q, k, v, log_g))
    B, T, H, _ = q.shape
    dv = v.shape[-1]
    C = math.gcd(GLA_CHUNK, T)
    n = T // C

    def to_chunks(a):
        return a.reshape(B, n, C, H, a.shape[-1]).transpose(1, 0, 3, 2, 4)

    causal = jnp.tril(jnp.ones((C, C), dtype=bool))

    def step(S, inp):
        qi, ki, vi, gi = inp
        b = jnp.cumsum(gi, axis=-2)
        diff = b[..., :, None, :] - b[..., None, :, :]
        decay = jnp.exp(jnp.where(causal[:, :, None], diff, -jnp.inf))
        att = jnp.einsum('bhtd,bhsd,bhtsd->bhts', qi, ki, decay)
        o = jnp.einsum('bhts,bhse->bhte', att, vi) + jnp.einsum('bhtd,bhde->bhte', qi * jnp.exp(b), S)
        b_last = b[..., -1, :]
        S = S * jnp.exp(b_last)[..., None] + jnp.einsum('bhsd,bhse->bhde', ki * jnp.exp(b_last[..., None, :] - b), vi)
        return S, o

    S, o = lax.scan(step, s0.astype(F32), (to_chunks(q), to_chunks(k), to_chunks(v), to_chunks(log_g)))
    o = o.transpose(1, 0, 3, 2, 4).reshape(B, T, H, dv)
    return o, S


def _hgrn2(hq, hf, hi, hg, lb, norm_w, s0):
    B, T, _ = hq.shape
    heads = lambda t: t.reshape(B, T, GROUP_HEADS, -1)
    hf = hf.astype(F32)
    q = jax.nn.silu(hq.astype(F32))
    log_f = jnp.logaddexp(jnp.log(lb), jnp.log1p(-lb) + jax.nn.log_sigmoid(hf))
    k = (1.0 - lb) * jax.nn.sigmoid(-hf)
    o, S = _chunked_gla(heads(q), heads(k), heads(hi), heads(log_f), s0)
    o = _rms(o, norm_w) * jax.nn.silu(heads(hg.astype(F32)))
    return o.reshape(B, T, GROUP_WIDTH), S


def _gla(gq, gk, gv, ga, gg, a2, a_b, norm_w, s0):
    B, T, _ = gq.shape
    heads = lambda t: t.reshape(B, T, GROUP_HEADS, -1)
    log_a = jax.nn.log_sigmoid(ga.astype(F32) @ a2.astype(F32) + a_b.astype(F32)) / GLA_GATE_NORM
    q = heads(gq.astype(F32)) * GLA_DK ** -0.5
    o, S = _chunked_gla(q, heads(gk), heads(gv), heads(log_a), s0)
    o = _rms(o, norm_w) * jax.nn.silu(heads(gg.astype(F32)))
    return o.reshape(B, T, GROUP_WIDTH), S


def _rwkv7(p, shift0, s0, mu, w0, w2, a0, a2, g2, k_k, k_a, r_k, ln_w, ln_b):
    B, T, _ = p.shape
    p = p.astype(F32)
    prev = jnp.concatenate([shift0.astype(F32)[:, None], p[:, :-1]], axis=1)
    xs = p + (prev - p) * mu.astype(F32)
    r, k, v, w_in, a_in, g_in = _split(xs, (GROUP_WIDTH, GROUP_WIDTH, GROUP_WIDTH, RW_DECAY_RANK, RW_A_RANK, RW_GATE_RANK))
    w = -jax.nn.softplus(-(w0.astype(F32) + jnp.tanh(w_in) @ w2.astype(F32))) - 0.5
    decay = jnp.exp(-jnp.exp(w))
    a = jax.nn.sigmoid(a0.astype(F32) + a_in @ a2.astype(F32))
    g = jax.nn.sigmoid(g_in) @ g2.astype(F32)
    heads = lambda t: t.reshape(B, T, GROUP_HEADS, HEAD_DIM)
    kk = heads(k * k_k.astype(F32))
    kk = kk / jnp.maximum(jnp.sqrt(jnp.sum(kk * kk, axis=-1, keepdims=True)), 1e-12)
    k = k * (1.0 + (a - 1.0) * k_a.astype(F32))
    r, k, v, a, decay = heads(r), heads(k), heads(v), heads(a), heads(decay)

    def step(S, inp):
        r_t, w_t, k_t, v_t, kk_t, a_t = inp
        sa = jnp.einsum('bhij,bhj->bhi', S, -kk_t)
        S = S * w_t[:, :, None, :] + sa[..., None] * (kk_t * a_t)[:, :, None, :] + v_t[..., None] * k_t[:, :, None, :]
        return S, jnp.einsum('bhij,bhj->bhi', S, r_t)

    tm = lambda t: jnp.moveaxis(t, 1, 0)
    S, y = lax.scan(step, s0.astype(F32), (tm(r), tm(decay), tm(k), tm(v), tm(kk), tm(a)))
    y = jnp.moveaxis(y, 0, 1)
    mean = jnp.mean(y, axis=-1, keepdims=True)
    var = jnp.mean(jnp.square(y - mean), axis=-1, keepdims=True)
    y = (y - mean) * lax.rsqrt(var + RW_GN_EPS) * ln_w.astype(F32).reshape(GROUP_HEADS, HEAD_DIM) \
        + ln_b.astype(F32).reshape(GROUP_HEADS, HEAD_DIM)
    y = y + jnp.sum(r * k * r_k.astype(F32), axis=-1, keepdims=True) * v
    out = y.reshape(B, T, GROUP_WIDTH) * g
    return out, S, p[:, -1].astype(shift0.dtype)


def _stick_breaking(q, k, v, q_pos0, bias):
    Tq, Tk = q.shape[1], k.shape[1]
    z = jnp.einsum('bqhd,bkhd->bhqk', q.astype(F32), k.astype(F32)) * HEAD_DIM ** -0.5 \
        + bias.astype(F32)[None, :, None, None]
    qpos = q_pos0 + jnp.arange(Tq)
    kpos = jnp.arange(Tk)
    mask = kpos[None, :] < qpos[:, None]
    log_1m = jnp.where(mask, jax.nn.log_sigmoid(-z), 0.0)
    between = lax.cumsum(log_1m, axis=3, reverse=True) - log_1m
    weights = jnp.exp(jnp.where(mask, jax.nn.log_sigmoid(z) + between, -jnp.inf))
    return jnp.einsum('bhqk,bkhd->bqhd', weights, v.astype(F32))


def _sb_blocks(q, k_all, v_all, pos0, bias):
    T = q.shape[1]
    outs = []
    for b0 in range(0, T, SB_BLOCK):
        b1 = min(b0 + SB_BLOCK, T)
        n_keys = pos0 + b1
        outs.append(_stick_breaking(q[:, b0:b1], k_all[:, :n_keys], v_all[:, :n_keys], pos0 + b0, bias))
    return jnp.concatenate(outs, axis=1)


def _layer(x, pos0, lb, hg_s0, rw_s0, rw_shift0, gla_s0, past_k, past_v, P):
    B, T, _ = x.shape
    h = _rms(x, P['norm_mix_pre'])
    proj = jnp.einsum('btd,dc->btc', h, P['w_in'])
    hq, hf, hi, hg, rw, sq, sk, sv, gq, gk, gv, ga, gg = _split(proj, IN_WIDTHS)
    o_hg, hg_S = _hgrn2(hq, hf, hi, hg, lb, P['hg_norm'], hg_s0)
    o_rw, rw_S, rw_shift = _rwkv7(rw, rw_shift0, rw_s0, P['rw_mu'], P['rw_w0'], P['rw_w2'], P['rw_a0'], P['rw_a2'],
                                  P['rw_g2'], P['rw_k_k'], P['rw_k_a'], P['rw_r_k'], P['rw_ln_w'], P['rw_ln_b'])
    q_h = sq.reshape(B, T, GROUP_HEADS, HEAD_DIM)
    k_h = sk.reshape(B, T, GROUP_HEADS, HEAD_DIM)
    v_h = sv.reshape(B, T, GROUP_HEADS, HEAD_DIM)
    if past_k is None:
        k_all, v_all = k_h, v_h
    else:
        k_all = jnp.concatenate([past_k.astype(k_h.dtype), k_h], axis=1)
        v_all = jnp.concatenate([past_v.astype(v_h.dtype), v_h], axis=1)
    o_sb = _sb_blocks(q_h, k_all, v_all, pos0, P['sb_bias']).reshape(B, T, GROUP_WIDTH)
    o_gla, gla_S = _gla(gq, gk, gv, ga, gg, P['gla_a2'], P['gla_a_b'], P['gla_norm'], gla_s0)
    mix = jnp.concatenate([o_hg, o_rw, o_sb, o_gla], axis=-1).astype(x.dtype)
    x = x + _rms(jnp.einsum('btc,cd->btd', mix, P['w_out']), P['norm_mix_post'])
    h2 = _rms(x, P['norm_ffn_pre'])
    u = jnp.square(jax.nn.relu(jnp.einsum('btd,df->btf', h2, P['w_up'])))
    x = x + _rms(jnp.einsum('btf,fd->btd', u, P['w_down']), P['norm_ffn_post'])
    return (x, k_h, v_h, hg_S.astype(hg_s0.dtype), rw_S.astype(rw_s0.dtype), rw_shift, gla_S.astype(gla_s0.dtype))


def setup_inputs(seed: int = 0) -> dict:
    key = jax.random.key(seed)
    ks = iter(jax.random.split(key, 48))
    nrm = lambda shape, s: jax.random.normal(next(ks), shape, jnp.float32) * s
    n_pages = PAST_LEN // PAGE_SIZE
    n_used = DEC_BATCH * n_pages
    n_pool = n_used + max(1, n_used // 4)
    H = GROUP_HEADS
    x_prompt = nrm((BATCH, SEQ, D_MODEL), 1.0)
    x_sample = nrm((DEC_BATCH, DEC_SEQ, D_MODEL), 1.0)
    cache_sb_k = nrm((DEPTH, n_pool, PAGE_SIZE, H, HEAD_DIM), 1.0)
    cache_sb_v = nrm((DEPTH, n_pool, PAGE_SIZE, H, HEAD_DIM), 1.0)
    page_table = jax.random.permutation(next(ks), n_pool)[:n_used].reshape(DEC_BATCH, n_pages).astype(jnp.int32)
    state_hgrn = nrm((DEPTH, DEC_BATCH, H, HG_DK, HG_DV), 0.5)
    state_rwkv = nrm((DEPTH, DEC_BATCH, H, HEAD_DIM, HEAD_DIM), 0.5)
    state_rwkv_shift = nrm((DEPTH, DEC_BATCH, RW_PROJ), 1.0)
    state_gla = nrm((DEPTH, DEC_BATCH, H, GLA_DK, GLA_DV), 0.5)
    gain = lambda shape: 1.0 + nrm(shape, 0.05)
    return {
        'x_prompt': x_prompt, 'x_sample': x_sample,
        'cache_sb_k': cache_sb_k, 'cache_sb_v': cache_sb_v, 'page_table': page_table,
        'state_hgrn': state_hgrn, 'state_rwkv': state_rwkv, 'state_rwkv_shift': state_rwkv_shift, 'state_gla': state_gla,
        'norm_mix_pre': gain((DEPTH, D_MODEL)), 'norm_mix_post': gain((DEPTH, D_MODEL)),
        'norm_ffn_pre': gain((DEPTH, D_MODEL)), 'norm_ffn_post': gain((DEPTH, D_MODEL)),
        'w_in': nrm((DEPTH, D_MODEL, IN_COLS), D_MODEL ** -0.5),
        'w_out': nrm((DEPTH, D_MODEL, D_MODEL), D_MODEL ** -0.5),
        'w_up': nrm((DEPTH, D_MODEL, D_FF), D_MODEL ** -0.5),
        'w_down': nrm((DEPTH, D_FF, D_MODEL), D_FF ** -0.5),
        'hg_lb': nrm((DEPTH, GROUP_WIDTH), 1.0),
        'hg_norm': gain((DEPTH, HG_DV)),
        'rw_mu': jax.random.uniform(next(ks), (DEPTH, RW_PROJ), jnp.float32),
        'rw_w0': jax.random.uniform(next(ks), (DEPTH, GROUP_WIDTH), jnp.float32, -6.0, 0.0),
        'rw_w2': nrm((DEPTH, RW_DECAY_RANK, GROUP_WIDTH), 0.1),
        'rw_a0': nrm((DEPTH, GROUP_WIDTH), 0.1),
        'rw_a2': nrm((DEPTH, RW_A_RANK, GROUP_WIDTH), 0.5 * RW_A_RANK ** -0.5),
        'rw_g2': nrm((DEPTH, RW_GATE_RANK, GROUP_WIDTH), RW_GATE_RANK ** -0.5),
        'rw_k_k': gain((DEPTH, GROUP_WIDTH)),
        'rw_k_a': gain((DEPTH, GROUP_WIDTH)),
        'rw_r_k': nrm((DEPTH, GROUP_HEADS, HEAD_DIM), 0.1),
        'rw_ln_w': gain((DEPTH, GROUP_WIDTH)),
        'rw_ln_b': nrm((DEPTH, GROUP_WIDTH), 0.01),
        'sb_bias': SB_BIAS_INIT + nrm((DEPTH, GROUP_HEADS), 0.1),
        'gla_a2': nrm((DEPTH, GLA_GATE_RANK, GROUP_HEADS * GLA_DK), GLA_GATE_RANK ** -0.5),
        'gla_a_b': nrm((DEPTH, GROUP_HEADS * GLA_DK), 0.1),
        'gla_norm': gain((DEPTH, GLA_DV)),
    }


def reference(x_prompt, x_sample, cache_sb_k, cache_sb_v, page_table, state_hgrn, state_rwkv, state_rwkv_shift,
              state_gla, norm_mix_pre, norm_mix_post, norm_ffn_pre, norm_ffn_post, w_in, w_out, w_up, w_down,
              hg_lb, hg_norm, rw_mu, rw_w0, rw_w2, rw_a0, rw_a2, rw_g2, rw_k_k, rw_k_a, rw_r_k, rw_ln_w, rw_ln_b,
              sb_bias, gla_a2, gla_a_b, gla_norm):
    lb_all = jnp.cumsum(jax.nn.softmax(hg_lb.astype(F32), axis=0), axis=0)
    lb_all = lb_all - lb_all[0:1]
    n_seq_p = x_prompt.shape[0]
    n_seq_s, n_pages = page_table.shape
    xp, xs = x_prompt, x_sample
    sbk_p, sbv_p, sbk_s, sbv_s = [], [], [], []
    hg_p, hg_s, rw_p, rw_s, sh_p, sh_s, gl_p, gl_s = [], [], [], [], [], [], [], []
    for l in range(DEPTH):
        P = {'norm_mix_pre': norm_mix_pre[l], 'norm_mix_post': norm_mix_post[l], 'norm_ffn_pre': norm_ffn_pre[l],
             'norm_ffn_post': norm_ffn_post[l], 'w_in': w_in[l], 'w_out': w_out[l], 'w_up': w_up[l],
             'w_down': w_down[l], 'hg_norm': hg_norm[l], 'rw_mu': rw_mu[l], 'rw_w0': rw_w0[l], 'rw_w2': rw_w2[l],
             'rw_a0': rw_a0[l], 'rw_a2': rw_a2[l], 'rw_g2': rw_g2[l], 'rw_k_k': rw_k_k[l], 'rw_k_a': rw_k_a[l],
             'rw_r_k': rw_r_k[l], 'rw_ln_w': rw_ln_w[l], 'rw_ln_b': rw_ln_b[l], 'sb_bias': sb_bias[l],
             'gla_a2': gla_a2[l], 'gla_a_b': gla_a_b[l], 'gla_norm': gla_norm[l]}
        lb = lb_all[l]
        xp, k_p, v_p, hS_p, rS_p, shf_p, gS_p = _layer(
            xp, 0, lb,
            jnp.zeros((n_seq_p, GROUP_HEADS, HG_DK, HG_DV), F32),
            jnp.zeros((n_seq_p, GROUP_HEADS, HEAD_DIM, HEAD_DIM), F32),
            jnp.zeros((n_seq_p, RW_PROJ), xp.dtype),
            jnp.zeros((n_seq_p, GROUP_HEADS, GLA_DK, GLA_DV), F32),
            None, None, P)
        past_k = cache_sb_k[l][page_table].reshape(n_seq_s, n_pages * PAGE_SIZE, GROUP_HEADS, HEAD_DIM)
        past_v = cache_sb_v[l][page_table].reshape(n_seq_s, n_pages * PAGE_SIZE, GROUP_HEADS, HEAD_DIM)
        xs, k_s, v_s, hS_s, rS_s, shf_s, gS_s = _layer(
            xs, n_pages * PAGE_SIZE, lb, state_hgrn[l], state_rwkv[l], state_rwkv_shift[l], state_gla[l],
            past_k, past_v, P)
        sbk_p.append(k_p); sbv_p.append(v_p); sbk_s.append(k_s); sbv_s.append(v_s)
        hg_p.append(hS_p); hg_s.append(hS_s); rw_p.append(rS_p); rw_s.append(rS_s)
        sh_p.append(shf_p); sh_s.append(shf_s); gl_p.append(gS_p); gl_s.append(gS_s)
    y_prompt, y_sample = xp, xs
    sb_k_prompt = jnp.stack(sbk_p, 0)
    sb_v_prompt = jnp.stack(sbv_p, 0)
    sb_k_sample = jnp.stack(sbk_s, 0)
    sb_v_sample = jnp.stack(sbv_s, 0)
    hgrn_prompt = jnp.stack(hg_p, 0)
    hgrn_sample = jnp.stack(hg_s, 0)
    rwkv_prompt = jnp.stack(rw_p, 0)
    rwkv_sample = jnp.stack(rw_s, 0)
    rwkv_shift_prompt = jnp.stack(sh_p, 0)
    rwkv_shift_sample = jnp.stack(sh_s, 0)
    gla_prompt = jnp.stack(gl_p, 0)
    gla_sample = jnp.stack(gl_s, 0)
    return (y_prompt, y_sample, sb_k_prompt, sb_v_prompt, sb_k_sample, sb_v_sample, hgrn_prompt, hgrn_sample,
            rwkv_prompt, rwkv_sample, rwkv_shift_prompt, rwkv_shift_sample, gla_prompt, gla_sample)
```

```python
import functools

import jax
import jax.numpy as jnp
import numpy as np
from jax import lax
from jax.experimental import pallas as pl
from jax.experimental.pallas import tpu as pltpu

F32 = jnp.float32
BF16 = jnp.bfloat16

D_MODEL = 1024
HEADS = 4
HEAD_DIM = 64
GW = HEADS * HEAD_DIM
GLA_DK = 32
GLA_KW = HEADS * GLA_DK
GLA_RANK = 16
GLA_GATE_NORM = 16.0
RW_PROJ = 896
RW_GN_EPS = 64e-5
D_FF = 4096
RMS_EPS = 1e-6
PAGE = 128
LANES = 128
KSEL = 64
VMEM_LIMIT = 52 * 1024 * 1024

HG_W = 4 * GW
SB_W = 3 * GW
GLA_W = 896


def _cparams(sem):
    return pltpu.CompilerParams(dimension_semantics=sem, vmem_limit_bytes=VMEM_LIMIT)


def _split2(x):
    hi = x.astype(BF16)
    lo = (x - hi.astype(F32)).astype(BF16)
    return hi, lo


def _dot(a, b):
    return jnp.dot(a, b, preferred_element_type=F32)


def _dot_exact_rhs(x, e):
    hi, lo = _split2(x)
    return _dot(hi, e) + _dot(lo, e)


def _dot3(a, b):
    ah, al = _split2(a)
    bh, bl = _split2(b)
    return _dot(ah, bh) + (_dot(ah, bl) + _dot(al, bh))


def _sigmoid(x):
    return 1.0 / (1.0 + jnp.exp(-x))


def _softplus(x):
    return jnp.maximum(x, 0.0) + jnp.log1p(jnp.exp(-jnp.abs(x)))


def _rms_rows(x, g):
    return x * lax.rsqrt(jnp.mean(x * x, axis=-1, keepdims=True) + RMS_EPS) * g


def _proj_kernel(x_ref, g_ref, w_ref, hg_ref, rw_ref, sb_ref, gla_ref):
    h = _rms_rows(x_ref[...], g_ref[...]).astype(BF16)
    c0 = 0
    for ref, width in ((hg_ref, HG_W), (rw_ref, RW_PROJ), (sb_ref, SB_W), (gla_ref, GLA_W)):
        ref[...] = _dot(h, w_ref[:, c0:c0 + width])
        c0 += width


def _proj(x, g, w, tm):
    n = x.shape[0]
    wtot = HG_W + RW_PROJ + SB_W + GLA_W
    row = lambda i: (i, 0)
    fixed = lambda i: (0, 0)
    return pl.pallas_call(
        _proj_kernel,
        grid=(n // tm,),
        in_specs=[pl.BlockSpec((tm, D_MODEL), row), pl.BlockSpec((1, D_MODEL), fixed),
                  pl.BlockSpec((D_MODEL, wtot), fixed)],
        out_specs=[pl.BlockSpec((tm, HG_W), row), pl.BlockSpec((tm, RW_PROJ), row),
                   pl.BlockSpec((tm, SB_W), row), pl.BlockSpec((tm, GLA_W), row)],
        out_shape=[jax.ShapeDtypeStruct((n, HG_W), F32), jax.ShapeDtypeStruct((n, RW_PROJ), F32),
                   jax.ShapeDtypeStruct((n, SB_W), F32), jax.ShapeDtypeStruct((n, GLA_W), F32)],
        compiler_params=_cparams(("parallel",)),
        name="in_proj",
    )(x, g, w)


def _store_split(ref, vecs, width):
    c = 0
    for x in vecs:
        hi, lo = _split2(x)
        ref[:, c:c + width] = hi
        ref[:, c + width:c + 2 * width] = lo
        c += 2 * width
    if c < ref.shape[1]:
        ref[:, c:] = jnp.zeros((ref.shape[0], ref.shape[1] - c), BF16)


def _prep_kernel(hg_ref, rw_ref, prev_ref, glaqk_ref, glaga_ref, e_ref, lb_ref, mu_ref, w0_ref, a0_ref,
                 kk_ref, ka_ref, rk_ref, w2_ref, a2_ref, g2_ref, ga2_ref, gab_ref,
                 rwsrc_ref, rwv_ref, rwg_ref, rwrkv_ref, hgsrc_ref, glasrc_ref):
    e = e_ref[...]
    hq = hg_ref[:, 0:GW]
    hf = hg_ref[:, GW:2 * GW]
    lb = lb_ref[...]
    f = lb + (1.0 - lb) * _sigmoid(hf)
    kf = (1.0 - lb) * _sigmoid(-hf)
    _store_split(hgsrc_ref, (f, kf, hq * _sigmoid(hq)), GW)

    p = rw_ref[...]
    xs = p + (prev_ref[...] - p) * mu_ref[...]
    r = xs[:, 0:GW]
    k = xs[:, GW:2 * GW]
    v = xs[:, 2 * GW:3 * GW]
    lora = xs[:, 3 * GW:RW_PROJ]
    w = -_softplus(-(w0_ref[...] + _dot3(jnp.tanh(lora), w2_ref[...]))) - 0.5
    decay = jnp.exp(-jnp.exp(w))
    a = _sigmoid(a0_ref[...] + _dot3(lora, a2_ref[...]))
    g = _dot3(_sigmoid(lora), g2_ref[...])
    kk = k * kk_ref[...]
    kk = kk / jnp.maximum(jnp.sqrt(_dot_exact_rhs(kk * kk, e)), 1e-12)
    kp = k * (1.0 + (a - 1.0) * ka_ref[...])
    rwv_ref[...] = v
    rwg_ref[...] = g
    rwrkv_ref[...] = _dot_exact_rhs(r * kp * rk_ref[...], e) * v
    _store_split(rwsrc_ref, (kk, decay, kk * a, kp, r), GW)

    gq = glaqk_ref[:, 0:GLA_KW]
    gk = glaqk_ref[:, GLA_KW:2 * GLA_KW]
    xg = _dot3(glaga_ref[...], ga2_ref[...]) + gab_ref[...]
    dec = jnp.exp(-_softplus(-xg) / GLA_GATE_NORM)
    _store_split(glasrc_ref, (dec, gk, gq * (GLA_DK ** -0.5)), GLA_KW)


def _prep(proj_hg, proj_rw, prev_rw, proj_gla, e256, lb, P, tm):
    n = proj_hg.shape[0]
    row = lambda i: (i, 0)
    fixed = lambda i: (0, 0)
    vec = lambda wd: pl.BlockSpec((1, wd), fixed)
    in_specs = [
        pl.BlockSpec((tm, 2 * GW), row),
        pl.BlockSpec((tm, RW_PROJ), row),
        pl.BlockSpec((tm, RW_PROJ), row),
        pl.BlockSpec((tm, 2 * GLA_KW), row),
        pl.BlockSpec((tm, LANES), lambda i: (i, 6)),
        pl.BlockSpec((GW, GW), fixed),
        vec(GW), vec(RW_PROJ), vec(GW), vec(GW), vec(GW), vec(GW), vec(GW),
        pl.BlockSpec((LANES, GW), fixed), pl.BlockSpec((LANES, GW), fixed), pl.BlockSpec((LANES, GW), fixed),
        pl.BlockSpec((LANES, GLA_KW), fixed), vec(GLA_KW),
    ]
    out_shapes = [
        jax.ShapeDtypeStruct((n, KSEL * HEAD_DIM), BF16),
        jax.ShapeDtypeStruct((n, GW), F32), jax.ShapeDtypeStruct((n, GW), F32), jax.ShapeDtypeStruct((n, GW), F32),
        jax.ShapeDtypeStruct((n, KSEL * HEAD_DIM), BF16),
        jax.ShapeDtypeStruct((n, KSEL * GLA_DK), BF16),
    ]
    out_specs = [pl.BlockSpec((tm, s.shape[1]), row) for s in out_shapes]
    return pl.pallas_call(
        _prep_kernel, grid=(n // tm,), in_specs=in_specs, out_specs=out_specs, out_shape=out_shapes,
        compiler_params=_cparams(("parallel",)), name="mixer_prep",
    )(proj_hg, proj_rw, prev_rw, proj_gla, proj_gla, e256, lb, P['rw_mu'], P['rw_w0'], P['rw_a0'],
      P['rw_k_k'], P['rw_k_a'], P['rw_r_k'], P['rw_w2p'], P['rw_a2p'], P['rw_g2p'], P['gla_a2p'], P['gla_a_b'])


def _rec_kernel(*refs, mode, nb, tb):
    lt_refs = refs[0:nb]
    v_refs = refs[nb:2 * nb]
    sel_ref, s0_ref, y_ref, sout_ref = refs[2 * nb:]
    ti = pl.program_id(1)

    @pl.when(ti == 0)
    def _():
        sout_ref[...] = s0_ref[...]

    def body(t, carry):
        out = []
        for c in range(nb):
            s = carry[c]
            z = _dot(lt_refs[c][t], sel_ref[...])
            v = v_refs[c][pl.ds(t, 1), :]
            if mode == 'rwkv':
                kk, w, kka, k, r = (z[:, i * GW:(i + 1) * GW] for i in range(5))
                sa = -jnp.sum(s * kk, axis=0, keepdims=True)
                s = s * w + kka * sa + k * v
                y = jnp.sum(s * r, axis=0, keepdims=True)
            else:
                g, k, q = (z[:, i * GW:(i + 1) * GW] for i in range(3))
                s = s * g + k * v
                y = jnp.sum(s * q, axis=0, keepdims=True)
            y_ref[c, pl.ds(t, 1), :] = y
            out.append(s)
        return tuple(out)

    fin = lax.fori_loop(0, tb, body, tuple(sout_ref[c] for c in range(nb)))
    for c in range(nb):
        sout_ref[c] = fin[c]


def _recurrence(lt, vsrc, vcol, sel, s0, *, mode, tok0, bsz, tlen, nb, tb):
    n, dk, _ = lt.shape
    nt = tlen // tb
    off = tok0 // tb
    lt4 = lt.reshape(n // tb, tb, dk, KSEL)
    v3 = vsrc.reshape(n // tb, tb, vsrc.shape[1])
    nsel = sel.shape[1]

    def tok_map(c, ncol):
        if ncol is None:
            return lambda bi, ti: (off + (bi * nb + c) * nt + ti, 0, 0, 0)
        return lambda bi, ti: (off + (bi * nb + c) * nt + ti, 0, ncol)

    in_specs = [pl.BlockSpec((None, tb, dk, KSEL), tok_map(c, None)) for c in range(nb)]
    in_specs += [pl.BlockSpec((None, tb, GW), tok_map(c, vcol)) for c in range(nb)]
    in_specs += [pl.BlockSpec((KSEL, nsel), lambda bi, ti: (0, 0)),
                 pl.BlockSpec((nb, dk, GW), lambda bi, ti: (bi, 0, 0))]
    out_specs = [pl.BlockSpec((nb, tb, GW), lambda bi, ti: (bi, ti, 0)),
                 pl.BlockSpec((nb, dk, GW), lambda bi, ti: (bi, 0, 0))]
    return pl.pallas_call(
        functools.partial(_rec_kernel, mode=mode, nb=nb, tb=tb),
        grid=(bsz // nb, nt), in_specs=in_specs, out_specs=out_specs,
        out_shape=[jax.ShapeDtypeStruct((bsz, tlen, GW), F32), jax.ShapeDtypeStruct((bsz, dk, GW), F32)],
        compiler_params=_cparams(("parallel", "arbitrary")), name="rec_" + mode,
    )(*([lt4] * nb), *([v3] * nb), sel, s0)


def _log1m_sigmoid(z):
    return jnp.minimum(-z, 0.0) - jnp.log1p(jnp.exp(-jnp.abs(z)))


def _sb_block(z, mask, u, carry, rowsum_lanes):
    l1m = _log1m_sigmoid(z)
    if mask is not None:
        l1m = jnp.where(mask, l1m, 0.0)
    hi, lo = _split2(l1m)
    between = carry + (_dot(hi, u) + _dot(lo, u))
    logw = z + l1m + between
    if mask is not None:
        logw = jnp.where(mask, logw, -jnp.inf)
    wgt = jnp.exp(logw)
    if rowsum_lanes is None:
        total = jnp.sum(l1m, axis=1, keepdims=True)
    else:
        total = _dot(hi, rowsum_lanes) + _dot(lo, rowsum_lanes)
    return wgt, carry + total


def _sbp_kernel(bias_ref, q_ref, k_ref, v_ref, u_ref, o_ref, *, tq):
    h = pl.program_id(1)
    qi = pl.program_id(2)
    bias = bias_ref[h]
    q = (q_ref[...] * (HEAD_DIM ** -0.5)).astype(BF16)
    u = u_ref[...]
    nt_dims = (((1,), (1,)), ((), ()))

    def block(j, carry, acc, mask):
        start = pl.multiple_of(j * tq, tq)
        kb = k_ref[pl.ds(start, tq), :].astype(BF16)
        vb = v_ref[pl.ds(start, tq), :].astype(BF16)
        z = lax.dot_general(q, kb, nt_dims, preferred_element_type=F32) + bias
        wgt, carry = _sb_block(z, mask, u, carry, None)
        return carry, acc + _dot(wgt.astype(BF16), vb)

    rows = lax.broadcasted_iota(jnp.int32, (tq, tq), 0)
    cols = lax.broadcasted_iota(jnp.int32, (tq, tq), 1)
    carry, acc = block(qi, jnp.zeros((tq, 1), F32), jnp.zeros((tq, HEAD_DIM), F32), cols < rows)

    def loop_body(i, ca):
        return block(qi - 1 - i, ca[0], ca[1], None)

    carry, acc = lax.fori_loop(0, qi, loop_body, (carry, acc))
    o_ref[...] = acc


def _sb_prompt(q, k, v, bias, u, tq):
    b, h, t, d = q.shape
    grid_spec = pltpu.PrefetchScalarGridSpec(
        num_scalar_prefetch=1, grid=(b, h, t // tq),
        in_specs=[pl.BlockSpec((None, None, tq, d), lambda bi, hi, qi, bias: (bi, hi, qi, 0)),
                  pl.BlockSpec((None, None, t, d), lambda bi, hi, qi, bias: (bi, hi, 0, 0)),
                  pl.BlockSpec((None, None, t, d), lambda bi, hi, qi, bias: (bi, hi, 0, 0)),
                  pl.BlockSpec((tq, tq), lambda bi, hi, qi, bias: (0, 0))],
        out_specs=pl.BlockSpec((None, None, tq, d), lambda bi, hi, qi, bias: (bi, hi, qi, 0)))
    return pl.pallas_call(
        functools.partial(_sbp_kernel, tq=tq), grid_spec=grid_spec,
        out_shape=jax.ShapeDtypeStruct((b, h, t, d), F32),
        compiler_params=_cparams(("parallel", "parallel", "arbitrary")), name="sb_prompt",
    )(bias, q, k, v, u)


SBS_PAGES = 4


def _sbs_kernel(pt_ref, bias_ref, q_ref, kn_ref, vn_ref, *rest, tnew):
    kp_refs = rest[0:SBS_PAGES]
    vp_refs = rest[SBS_PAGES:2 * SBS_PAGES]
    u_ref, ones_ref, o_ref, acc_ref, carry_ref, knew_ref, vnew_ref = rest[2 * SBS_PAGES:]
    s = pl.program_id(1)
    rows_n = HEADS * tnew
    u = u_ref[...]
    ones = ones_ref[...]
    nt_dims = (((1,), (1,)), ((), ()))

    row_head = lax.broadcasted_iota(jnp.int32, (rows_n, GW), 0) // tnew
    lane_head = lax.broadcasted_iota(jnp.int32, (rows_n, GW), 1) // HEAD_DIM
    q = q_ref[...] * (HEAD_DIM ** -0.5)
    qbd = jnp.where(row_head == lane_head, jnp.concatenate([q] * HEADS, axis=0), 0.0).astype(BF16)
    rh = lax.broadcasted_iota(jnp.int32, (rows_n, PAGE), 0) // tnew
    bias = jnp.zeros((rows_n, PAGE), F32)
    for hh in range(HEADS):
        bias = jnp.where(rh == hh, bias_ref[hh], bias)

    def block(kb, vb, mask):
        z = lax.dot_general(qbd, kb.astype(BF16), nt_dims, preferred_element_type=F32) + bias
        wgt, carry = _sb_block(z, mask, u, carry_ref[...], ones)
        carry_ref[...] = carry
        acc_ref[...] += _dot(wgt.astype(BF16), vb.astype(BF16))

    @pl.when(s == 0)
    def _():
        acc_ref[...] = jnp.zeros_like(acc_ref)
        carry_ref[...] = jnp.zeros_like(carry_ref)
        knew_ref[...] = jnp.zeros_like(knew_ref)
        vnew_ref[...] = jnp.zeros_like(vnew_ref)
        knew_ref[0:tnew, :] = kn_ref[...]
        vnew_ref[0:tnew, :] = vn_ref[...]
        trow = lax.broadcasted_iota(jnp.int32, (rows_n, PAGE), 0) % tnew
        col = lax.broadcasted_iota(jnp.int32, (rows_n, PAGE), 1)
        block(knew_ref[...], vnew_ref[...], col < trow)

    for c in range(SBS_PAGES):
        block(kp_refs[c][...], vp_refs[c][...], None)

    @pl.when(s == pl.num_programs(1) - 1)
    def _():
        acc = acc_ref[...]
        lh = lax.broadcasted_iota(jnp.int32, (tnew, GW), 1) // HEAD_DIM
        out = jnp.zeros((tnew, GW), F32)
        for hh in range(HEADS):
            out = jnp.where(lh == hh, acc[hh * tnew:(hh + 1) * tnew, :], out)
        o_ref[...] = out


def _sb_sample(proj_sb, cache_k, cache_v, layer, page_table, bias, u, ones, *, tok0, bsz, tnew):
    n = proj_sb.shape[0]
    npages = page_table.shape[1]
    nsteps = npages // SBS_PAGES
    p3 = proj_sb.reshape(n // tnew, tnew, SB_W)
    off = tok0 // tnew
    tok = lambda col: (lambda b, s, pt, bias: (off + b, 0, col))

    def page(c):
        return lambda b, s, pt, bias: (layer, pt[b, npages - 1 - (s * SBS_PAGES + c)], 0, 0)

    fixed = lambda b, s, pt, bias: (0, 0)
    in_specs = [pl.BlockSpec((None, tnew, GW), tok(0)), pl.BlockSpec((None, tnew, GW), tok(1)),
                pl.BlockSpec((None, tnew, GW), tok(2))]
    in_specs += [pl.BlockSpec((None, None, PAGE, GW), page(c)) for c in range(SBS_PAGES)]
    in_specs += [pl.BlockSpec((None, None, PAGE, GW), page(c)) for c in range(SBS_PAGES)]
    in_specs += [pl.BlockSpec((PAGE, PAGE), fixed), pl.BlockSpec((PAGE, PAGE), fixed)]
    grid_spec = pltpu.PrefetchScalarGridSpec(
        num_scalar_prefetch=2, grid=(bsz, nsteps), in_specs=in_specs,
        out_specs=pl.BlockSpec((None, tnew, GW), lambda b, s, pt, bias: (b, 0, 0)),
        scratch_shapes=[pltpu.VMEM((HEADS * tnew, GW), F32), pltpu.VMEM((HEADS * tnew, PAGE), F32),
                        pltpu.VMEM((PAGE, GW), F32), pltpu.VMEM((PAGE, GW), F32)])
    return pl.pallas_call(
        functools.partial(_sbs_kernel, tnew=tnew), grid_spec=grid_spec,
        out_shape=jax.ShapeDtypeStruct((bsz, tnew, GW), F32),
        compiler_params=_cparams(("parallel", "arbitrary")), name="sb_sample",
    )(page_table, bias, p3, p3, p3, *([cache_k] * SBS_PAGES), *([cache_v] * SBS_PAGES), u, ones)


def _out_kernel(x_ref, yhg_ref, hgg_ref, yrw_ref, rwrkv_ref, rwg_ref, osb_ref, ygla_ref, glag_ref,
                e_ref, hgn_ref, lnw_ref, lnb_ref, glan_ref, post_ref, w_ref, o_ref):
    e = e_ref[...]
    inv = 1.0 / HEAD_DIM

    def head_rms(y, g):
        return y * lax.rsqrt(_dot_exact_rhs(y * y, e) * inv + RMS_EPS) * g

    def silu(t):
        return t * _sigmoid(t)

    o_hg = head_rms(yhg_ref[...], hgn_ref[...]) * silu(hgg_ref[...])
    y = yrw_ref[...]
    mean = _dot_exact_rhs(y, e) * inv
    yc = y - mean
    var = _dot_exact_rhs(yc * yc, e) * inv
    o_rw = (yc * lax.rsqrt(var + RW_GN_EPS) * lnw_ref[...] + lnb_ref[...] + rwrkv_ref[...]) * rwg_ref[...]
    o_gla = head_rms(ygla_ref[...], glan_ref[...]) * silu(glag_ref[...])
    acc = _dot(o_hg.astype(BF16), w_ref[0:GW, :])
    acc += _dot(o_rw.astype(BF16), w_ref[GW:2 * GW, :])
    acc += _dot(osb_ref[...].astype(BF16), w_ref[2 * GW:3 * GW, :])
    acc += _dot(o_gla.astype(BF16), w_ref[3 * GW:4 * GW, :])
    o_ref[...] = x_ref[...] + _rms_rows(acc, post_ref[...])


def _out_proj(x, y_hg, proj_hg, y_rw, rw_rkv, rw_g, o_sb, y_gla, proj_gla, e256, P, tm):
    n = x.shape[0]
    row = lambda i: (i, 0)
    fixed = lambda i: (0, 0)
    blk = pl.BlockSpec((tm, GW), row)
    vec = pl.BlockSpec((1, GW), fixed)
    in_specs = [pl.BlockSpec((tm, D_MODEL), row), blk, pl.BlockSpec((tm, GW), lambda i: (i, 3)),
                blk, blk, blk, blk, blk, pl.BlockSpec((tm, GW), lambda i: (i, 2)),
                pl.BlockSpec((GW, GW), fixed), vec, vec, vec, vec,
                pl.BlockSpec((1, D_MODEL), fixed), pl.BlockSpec((D_MODEL, D_MODEL), fixed)]
    return pl.pallas_call(
        _out_kernel, grid=(n // tm,), in_specs=in_specs, out_specs=pl.BlockSpec((tm, D_MODEL), row),
        out_shape=jax.ShapeDtypeStruct((n, D_MODEL), F32),
        compiler_params=_cparams(("parallel",)), name="out_proj",
    )(x, y_hg, proj_hg, y_rw, rw_rkv, rw_g, o_sb, y_gla, proj_gla, e256,
      P['hg_norm'], P['rw_ln_w'], P['rw_ln_b'], P['gla_norm'], P['norm_mix_post'], P['w_out'])


def _ffn_kernel(x_ref, gpre_ref, gpost_ref, wu_ref, wd_ref, o_ref, h_ref, acc_ref):
    f = pl.program_id(1)

    @pl.when(f == 0)
    def _():
        h_ref[...] = _rms_rows(x_ref[...], gpre_ref[...]).astype(BF16)
        acc_ref[...] = jnp.zeros_like(acc_ref)

    u = jnp.maximum(_dot(h_ref[...], wu_ref[...]), 0.0)
    acc_ref[...] += _dot((u * u).astype(BF16), wd_ref[...])

    @pl.when(f == pl.num_programs(1) - 1)
    def _():
        o_ref[...] = x_ref[...] + _rms_rows(acc_ref[...], gpost_ref[...])


def _ffn(x, gpre, gpost, wu, wd, tm, tf):
    n = x.shape[0]
    return pl.pallas_call(
        _ffn_kernel, grid=(n // tm, D_FF // tf),
        in_specs=[pl.BlockSpec((tm, D_MODEL), lambda i, f: (i, 0)),
                  pl.BlockSpec((1, D_MODEL), lambda i, f: (0, 0)), pl.BlockSpec((1, D_MODEL), lambda i, f: (0, 0)),
                  pl.BlockSpec((D_MODEL, tf), lambda i, f: (0, f)), pl.BlockSpec((tf, D_MODEL), lambda i, f: (f, 0))],
        out_specs=pl.BlockSpec((tm, D_MODEL), lambda i, f: (i, 0)),
        out_shape=jax.ShapeDtypeStruct((n, D_MODEL), F32),
        scratch_shapes=[pltpu.VMEM((tm, D_MODEL), BF16), pltpu.VMEM((tm, D_MODEL), F32)],
        compiler_params=_cparams(("parallel", "arbitrary")), name="ffn",
    )(x, gpre, gpost, wu, wd)


def _selector(nvec):
    k = np.arange(KSEL)
    col = np.arange(nvec * GW)
    vk, hk = k // (2 * HEADS), k % HEADS
    sel = (vk[:, None] == (col // GW)[None, :]) & (hk[:, None] == ((col % GW) // HEAD_DIM)[None, :])
    sel &= (k < nvec * 2 * HEADS)[:, None]
    return jnp.asarray(sel, BF16)


def _pad_rows(w, r0, rows):
    return jnp.zeros((rows, w.shape[1]), F32).at[r0:r0 + w.shape[0]].set(w.astype(F32))


def _layer_params(l, norm_mix_pre, norm_mix_post, norm_ffn_pre, norm_ffn_post, w_in, w_out, w_up, w_down,
                  hg_norm, rw_mu, rw_w0, rw_w2, rw_a0, rw_a2, rw_g2, rw_k_k, rw_k_a, rw_r_k, rw_ln_w, rw_ln_b,
                  sb_bias, gla_a2, gla_a_b, gla_norm):
    r = lambda a: a[l].astype(F32).reshape(1, -1)
    w = w_in[l]
    gla0 = HG_W + RW_PROJ + SB_W
    w_pad = jnp.concatenate(
        [w[:, :gla0 + 512], w[:, gla0 + 512 + GLA_RANK:], w[:, gla0 + 512:gla0 + 512 + GLA_RANK],
         jnp.zeros((D_MODEL, LANES - GLA_RANK), w.dtype)], axis=1).astype(BF16)
    return {
        'norm_mix_pre': r(norm_mix_pre), 'norm_mix_post': r(norm_mix_post),
        'norm_ffn_pre': r(norm_ffn_pre), 'norm_ffn_post': r(norm_ffn_post),
        'w_in': w_pad, 'w_out': w_out[l].astype(BF16), 'w_up': w_up[l].astype(BF16), 'w_down': w_down[l].astype(BF16),
        'hg_norm': jnp.tile(r(hg_norm), (1, HEADS)), 'gla_norm': jnp.tile(r(gla_norm), (1, HEADS)),
        'rw_mu': r(rw_mu), 'rw_w0': r(rw_w0), 'rw_a0': r(rw_a0), 'rw_k_k': r(rw_k_k), 'rw_k_a': r(rw_k_a),
        'rw_r_k': r(rw_r_k), 'rw_ln_w': r(rw_ln_w), 'rw_ln_b': r(rw_ln_b),
        'rw_w2p': _pad_rows(rw_w2[l], 0, LANES), 'rw_a2p': _pad_rows(rw_a2[l], 32, LANES),
        'rw_g2p': _pad_rows(rw_g2[l], 64, LANES),
        'gla_a2p': _pad_rows(gla_a2[l], 0, LANES), 'gla_a_b': r(gla_a_b),
        'sb_bias': sb_bias[l].astype(F32),
    }


def _tile(n, pref):
    t = pref
    while n % t:
        t //= 2
    return t


def _to_lhs(src, dk):
    return jnp.swapaxes(src.reshape(src.shape[0], KSEL, dk), 1, 2)


def _state_in(s):
    b, h, dk, dv = s.shape
    return jnp.transpose(s.astype(F32), (0, 2, 1, 3)).reshape(b, dk, h * dv)


def _state_out(s):
    b, dk, _ = s.shape
    return jnp.transpose(s.reshape(b, dk, HEADS, HEAD_DIM), (0, 2, 1, 3))


def kernel(x_prompt, x_sample, cache_sb_k, cache_sb_v, page_table, state_hgrn, state_rwkv, state_rwkv_shift,
           state_gla, norm_mix_pre, norm_mix_post, norm_ffn_pre, norm_ffn_post, w_in, w_out, w_up, w_down,
           hg_lb, hg_norm, rw_mu, rw_w0, rw_w2, rw_a0, rw_a2, rw_g2, rw_k_k, rw_k_a, rw_r_k, rw_ln_w, rw_ln_b,
           sb_bias, gla_a2, gla_a_b, gla_norm):
    depth = w_in.shape[0]
    bp, tp, _ = x_prompt.shape
    bs, ts, _ = x_sample.shape
    n_p, n_s = bp * tp, bs * ts
    n = n_p + n_s
    n_pool = cache_sb_k.shape[1]
    cache_k = cache_sb_k.reshape(depth, n_pool, PAGE, GW)
    cache_v = cache_sb_v.reshape(depth, n_pool, PAGE, GW)

    lb_all = jnp.cumsum(jax.nn.softmax(hg_lb.astype(F32), axis=0), axis=0)
    lb_all = lb_all - lb_all[0:1]
    e256 = jnp.asarray(np.arange(GW)[:, None] // HEAD_DIM == np.arange(GW)[None, :] // HEAD_DIM, BF16)
    sel5, sel3 = _selector(5), _selector(3)
    tq = min(256, tp)
    u_p = jnp.asarray(np.arange(tq)[:, None] > np.arange(tq)[None, :], BF16)
    u_s = jnp.asarray(np.arange(PAGE)[:, None] > np.arange(PAGE)[None, :], BF16)
    ones_s = jnp.ones((PAGE, PAGE), BF16)
    tm = _tile(n, 512)
    tm_prep = _tile(n, 256)
    tm_ffn = _tile(n, 1024)
    tb_p = min(128, tp)

    x = jnp.concatenate([x_prompt.reshape(n_p, D_MODEL), x_sample.reshape(n_s, D_MODEL)], axis=0)
    outs = {k: [] for k in ('sbk_p', 'sbv_p', 'sbk_s', 'sbv_s', 'hg_p', 'hg_s', 'rw_p', 'rw_s', 'sh_p', 'sh_s',
                            'gl_p', 'gl_s')}
    for l in range(depth):
        P = _layer_params(l, norm_mix_pre, norm_mix_post, norm_ffn_pre, norm_ffn_post, w_in, w_out, w_up, w_down,
                          hg_norm, rw_mu, rw_w0, rw_w2, rw_a0, rw_a2, rw_g2, rw_k_k, rw_k_a, rw_r_k, rw_ln_w,
                          rw_ln_b, sb_bias, gla_a2, gla_a_b, gla_norm)
        proj_hg, proj_rw, proj_sb, proj_gla = _proj(x, P['norm_mix_pre'], P['w_in'], tm)

        rw_p3 = proj_rw[:n_p].reshape(bp, tp, RW_PROJ)
        rw_s3 = proj_rw[n_p:].reshape(bs, ts, RW_PROJ)
        prev = jnp.concatenate([
            jnp.concatenate([jnp.zeros((bp, 1, RW_PROJ), F32), rw_p3[:, :-1]], axis=1).reshape(n_p, RW_PROJ),
            jnp.concatenate([state_rwkv_shift[l].astype(F32)[:, None], rw_s3[:, :-1]], axis=1).reshape(n_s, RW_PROJ),
        ], axis=0)
        rw_src, rw_v, rw_g, rw_rkv, hg_src, gla_src = _prep(
            proj_hg, proj_rw, prev, proj_gla, e256, lb_all[l].reshape(1, GW), P, tm_prep)
        rw_lt, hg_lt, gla_lt = _to_lhs(rw_src, HEAD_DIM), _to_lhs(hg_src, HEAD_DIM), _to_lhs(gla_src, GLA_DK)

        def rec(lt, vsrc, vcol, sel, mode, dk, s0_sample):
            y_p, s_p = _recurrence(lt, vsrc, vcol, sel, jnp.zeros((bp, dk, GW), F32), mode=mode, tok0=0,
                                   bsz=bp, tlen=tp, nb=2, tb=tb_p)
            y_s, s_s = _recurrence(lt, vsrc, vcol, sel, s0_sample, mode=mode, tok0=n_p,
                                   bsz=bs, tlen=ts, nb=2, tb=ts)
            return jnp.concatenate([y_p.reshape(n_p, GW), y_s.reshape(n_s, GW)], axis=0), s_p, s_s

        y_hg, hgS_p, hgS_s = rec(hg_lt, proj_hg, 2, sel3, 'gla', HEAD_DIM, _state_in(state_hgrn[l]))
        rw_s0 = _state_in(jnp.swapaxes(state_rwkv[l], -1, -2))
        y_rw, rwS_p, rwS_s = rec(rw_lt, rw_v, 0, sel5, 'rwkv', HEAD_DIM, rw_s0)
        y_gla, glS_p, glS_s = rec(gla_lt, proj_gla, 1, sel3, 'gla', GLA_DK, _state_in(state_gla[l]))

        sb_p = proj_sb[:n_p].reshape(bp, tp, 3, HEADS, HEAD_DIM)
        qkv_p = jnp.transpose(sb_p, (2, 0, 3, 1, 4))
        o_sb_p = _sb_prompt(qkv_p[0], qkv_p[1], qkv_p[2], P['sb_bias'], u_p, tq)
        o_sb_p = jnp.transpose(o_sb_p, (0, 2, 1, 3)).reshape(n_p, GW)
        o_sb_s = _sb_sample(proj_sb, cache_k, cache_v, l, page_table, P['sb_bias'], u_s, ones_s,
                            tok0=n_p, bsz=bs, tnew=ts).reshape(n_s, GW)
        o_sb = jnp.concatenate([o_sb_p, o_sb_s], axis=0)

        x = _out_proj(x, y_hg, proj_hg, y_rw, rw_rkv, rw_g, o_sb, y_gla, proj_gla, e256, P, tm)
        x = _ffn(x, P['norm_ffn_pre'], P['norm_ffn_post'], P['w_up'], P['w_down'], tm_ffn, 1024)

        sb_s = proj_sb[n_p:].reshape(bs, ts, 3, HEADS, HEAD_DIM)
        outs['sbk_p'].append(sb_p[:, :, 1]); outs['sbv_p'].append(sb_p[:, :, 2])
        outs['sbk_s'].append(sb_s[:, :, 1]); outs['sbv_s'].append(sb_s[:, :, 2])
        outs['hg_p'].append(_state_out(hgS_p)); outs['hg_s'].append(_state_out(hgS_s).astype(state_hgrn.dtype))
        outs['rw_p'].append(jnp.swapaxes(_state_out(rwS_p), -1, -2))
        outs['rw_s'].append(jnp.swapaxes(_state_out(rwS_s), -1, -2).astype(state_rwkv.dtype))
        outs['sh_p'].append(rw_p3[:, -1]); outs['sh_s'].append(rw_s3[:, -1].astype(state_rwkv_shift.dtype))
        outs['gl_p'].append(_state_out(glS_p)); outs['gl_s'].append(_state_out(glS_s).astype(state_gla.dtype))

    st = lambda k: jnp.stack(outs[k], 0)
    return (x[:n_p].reshape(bp, tp, D_MODEL), x[n_p:].reshape(bs, ts, D_MODEL),
            st('sbk_p'), st('sbv_p'), st('sbk_s'), st('sbv_s'), st('hg_p'), st('hg_s'),
            st('rw_p'), st('rw_s'), st('sh_p'), st('sh_s'), st('gl_p'), st('gl_s'))
```

```python
import functools

import jax
import jax.numpy as jnp
import numpy as np
from jax import lax
from jax.experimental import pallas as pl
from jax.experimental.pallas import tpu as pltpu

F32 = jnp.float32
BF16 = jnp.bfloat16

D_MODEL = 1024
HEADS = 4
HEAD_DIM = 64
GW = HEADS * HEAD_DIM
GLA_DK = 32
GLA_KW = HEADS * GLA_DK
GLA_RANK = 16
GLA_GATE_NORM = 16.0
RW_PROJ = 896
RW_GN_EPS = 64e-5
D_FF = 4096
RMS_EPS = 1e-6
PAGE = 128
LANES = 128
KSEL = 64
REC_UNROLL = 16
VMEM_LIMIT = 52 * 1024 * 1024

HG_W = 4 * GW
SB_W = 3 * GW
GLA_W = 896


def _cparams(sem):
    return pltpu.CompilerParams(dimension_semantics=sem, vmem_limit_bytes=VMEM_LIMIT)


def _split2(x):
    hi = x.astype(BF16)
    lo = (x - hi.astype(F32)).astype(BF16)
    return hi, lo


def _dot(a, b):
    return jnp.dot(a, b, preferred_element_type=F32)


def _dot_exact_rhs(x, e):
    hi, lo = _split2(x)
    return _dot(hi, e) + _dot(lo, e)


def _dot3(a, b):
    ah, al = _split2(a)
    bh, bl = _split2(b)
    return _dot(ah, bh) + (_dot(ah, bl) + _dot(al, bh))


def _sigmoid(x):
    return 1.0 / (1.0 + jnp.exp(-x))


def _softplus(x):
    return jnp.maximum(x, 0.0) + jnp.log1p(jnp.exp(-jnp.abs(x)))


def _rms_rows(x, g):
    return x * lax.rsqrt(jnp.mean(x * x, axis=-1, keepdims=True) + RMS_EPS) * g


def _proj_kernel(x_ref, g_ref, w_ref, hg_ref, rw_ref, sb_ref, gla_ref):
    h = _rms_rows(x_ref[...], g_ref[...]).astype(BF16)
    c0 = 0
    for ref, width in ((hg_ref, HG_W), (rw_ref, RW_PROJ), (sb_ref, SB_W), (gla_ref, GLA_W)):
        ref[...] = _dot(h, w_ref[:, c0:c0 + width])
        c0 += width


def _proj(x, g, w, tm):
    n = x.shape[0]
    wtot = HG_W + RW_PROJ + SB_W + GLA_W
    row = lambda i: (i, 0)
    fixed = lambda i: (0, 0)
    return pl.pallas_call(
        _proj_kernel,
        grid=(n // tm,),
        in_specs=[pl.BlockSpec((tm, D_MODEL), row), pl.BlockSpec((1, D_MODEL), fixed),
                  pl.BlockSpec((D_MODEL, wtot), fixed)],
        out_specs=[pl.BlockSpec((tm, HG_W), row), pl.BlockSpec((tm, RW_PROJ), row),
                   pl.BlockSpec((tm, SB_W), row), pl.BlockSpec((tm, GLA_W), row)],
        out_shape=[jax.ShapeDtypeStruct((n, HG_W), F32), jax.ShapeDtypeStruct((n, RW_PROJ), F32),
                   jax.ShapeDtypeStruct((n, SB_W), F32), jax.ShapeDtypeStruct((n, GLA_W), F32)],
        compiler_params=_cparams(("parallel",)),
        name="in_proj",
    )(x, g, w)


def _store_split(ref, vecs, width):
    c = 0
    for x in vecs:
        hi, lo = _split2(x)
        ref[:, c:c + width] = hi
        ref[:, c + width:c + 2 * width] = lo
        c += 2 * width
    if c < ref.shape[1]:
        ref[:, c:] = jnp.zeros((ref.shape[0], ref.shape[1] - c), BF16)


def _prep_kernel(hg_ref, rw_ref, tail_ref, prevs_ref, glaqk_ref, glaga_ref, e_ref, lb_ref, mu_ref, w0_ref, a0_ref,
                 kk_ref, ka_ref, rk_ref, w2_ref, a2_ref, g2_ref, ga2_ref, gab_ref,
                 rwsrc_ref, rwv_ref, rwg_ref, rwrkv_ref, hgsrc_ref, glasrc_ref, *, prompt_tiles, seq_tiles):
    e = e_ref[...]
    i = pl.program_id(0)
    hq = hg_ref[:, 0:GW]
    hf = hg_ref[:, GW:2 * GW]
    lb = lb_ref[...]
    f = lb + (1.0 - lb) * _sigmoid(hf)
    kf = (1.0 - lb) * _sigmoid(-hf)
    _store_split(hgsrc_ref, (f, kf, hq * _sigmoid(hq)), GW)

    p = rw_ref[...]
    row0 = jnp.where(i % seq_tiles == 0, 0.0, tail_ref[7:8, :])
    first = lax.broadcasted_iota(jnp.int32, p.shape, 0) == 0
    prev = jnp.where(first, row0, pltpu.roll(p, 1, axis=0))
    prev = jnp.where(i >= prompt_tiles, prevs_ref[...], prev)
    xs = p + (prev - p) * mu_ref[...]
    r = xs[:, 0:GW]
    k = xs[:, GW:2 * GW]
    v = xs[:, 2 * GW:3 * GW]
    lora = xs[:, 3 * GW:RW_PROJ]
    w = -_softplus(-(w0_ref[...] + _dot3(jnp.tanh(lora), w2_ref[...]))) - 0.5
    decay = jnp.exp(-jnp.exp(w))
    a = _sigmoid(a0_ref[...] + _dot3(lora, a2_ref[...]))
    g = _dot3(_sigmoid(lora), g2_ref[...])
    kk = k * kk_ref[...]
    kk = kk / jnp.maximum(jnp.sqrt(_dot_exact_rhs(kk * kk, e)), 1e-12)
    kp = k * (1.0 + (a - 1.0) * ka_ref[...])
    rwv_ref[...] = v
    rwg_ref[...] = g
    rwrkv_ref[...] = _dot_exact_rhs(r * kp * rk_ref[...], e) * v
    _store_split(rwsrc_ref, (kk, decay, kk * a, kp, r), GW)

    gq = glaqk_ref[:, 0:GLA_KW]
    gk = glaqk_ref[:, GLA_KW:2 * GLA_KW]
    xg = _dot3(glaga_ref[...], ga2_ref[...]) + gab_ref[...]
    dec = jnp.exp(-_softplus(-xg) / GLA_GATE_NORM)
    _store_split(glasrc_ref, (dec, gk, gq * (GLA_DK ** -0.5)), GLA_KW)


def _prep(proj_hg, proj_rw, prev_s, proj_gla, e256, lb, P, tm, n_p, tp):
    n = proj_hg.shape[0]
    assert tp % tm == 0 and n_p % tm == 0 and prev_s.shape[0] % tm == 0 and tm % 8 == 0
    prompt_tiles = n_p // tm
    row = lambda i: (i, 0)
    fixed = lambda i: (0, 0)
    vec = lambda wd: pl.BlockSpec((1, wd), fixed)
    in_specs = [
        pl.BlockSpec((tm, 2 * GW), row),
        pl.BlockSpec((tm, RW_PROJ), row),
        pl.BlockSpec((8, RW_PROJ), lambda i: (jnp.maximum(i * (tm // 8) - 1, 0), 0)),
        pl.BlockSpec((tm, RW_PROJ), lambda i: (jnp.maximum(i - prompt_tiles, 0), 0)),
        pl.BlockSpec((tm, 2 * GLA_KW), row),
        pl.BlockSpec((tm, LANES), lambda i: (i, 6)),
        pl.BlockSpec((GW, GW), fixed),
        vec(GW), vec(RW_PROJ), vec(GW), vec(GW), vec(GW), vec(GW), vec(GW),
        pl.BlockSpec((LANES, GW), fixed), pl.BlockSpec((LANES, GW), fixed), pl.BlockSpec((LANES, GW), fixed),
        pl.BlockSpec((LANES, GLA_KW), fixed), vec(GLA_KW),
    ]
    out_shapes = [
        jax.ShapeDtypeStruct((n, KSEL * HEAD_DIM), BF16),
        jax.ShapeDtypeStruct((n, GW), F32), jax.ShapeDtypeStruct((n, GW), F32), jax.ShapeDtypeStruct((n, GW), F32),
        jax.ShapeDtypeStruct((n, KSEL * HEAD_DIM), BF16),
        jax.ShapeDtypeStruct((n, KSEL * GLA_DK), BF16),
    ]
    out_specs = [pl.BlockSpec((tm, s.shape[1]), row) for s in out_shapes]
    return pl.pallas_call(
        functools.partial(_prep_kernel, prompt_tiles=prompt_tiles, seq_tiles=tp // tm),
        grid=(n // tm,), in_specs=in_specs, out_specs=out_specs, out_shape=out_shapes,
        compiler_params=_cparams(("parallel",)), name="mixer_prep",
    )(proj_hg, proj_rw, proj_rw, prev_s, proj_gla, proj_gla, e256, lb, P['rw_mu'], P['rw_w0'], P['rw_a0'],
      P['rw_k_k'], P['rw_k_a'], P['rw_r_k'], P['rw_w2p'], P['rw_a2p'], P['rw_g2p'], P['gla_a2p'], P['gla_a_b'])


def _rec_kernel(*refs, mode, nb, tb):
    lt_refs = refs[0:nb]
    v_refs = refs[nb:2 * nb]
    sel_ref, s0_ref, y_ref, sout_ref = refs[2 * nb:]
    ti = pl.program_id(1)

    @pl.when(ti == 0)
    def _():
        sout_ref[...] = s0_ref[...]

    unroll = min(REC_UNROLL, tb)

    def body(g, carry):
        out = list(carry)
        for u in range(unroll):
            t = g * unroll + u
            for c in range(nb):
                s = out[c]
                z = lax.dot_general(lt_refs[c][t], sel_ref[...], (((0,), (0,)), ((), ())),
                                    preferred_element_type=F32)
                v = v_refs[c][pl.ds(t, 1), :]
                if mode == 'rwkv':
                    kk, w, kka, k, r = (z[:, i * GW:(i + 1) * GW] for i in range(5))
                    sa = -jnp.sum(s * kk, axis=0, keepdims=True)
                    s = s * w + kka * sa + k * v
                    y = jnp.sum(s * r, axis=0, keepdims=True)
                else:
                    g_, k, q = (z[:, i * GW:(i + 1) * GW] for i in range(3))
                    s = s * g_ + k * v
                    y = jnp.sum(s * q, axis=0, keepdims=True)
                y_ref[c, pl.ds(t, 1), :] = y
                out[c] = s
        return tuple(out)

    fin = lax.fori_loop(0, tb // unroll, body, tuple(sout_ref[c] for c in range(nb)))
    for c in range(nb):
        sout_ref[c] = fin[c]


def _recurrence(lt, vsrc, vcol, sel, s0, *, mode, tok0, bsz, tlen, nb, tb):
    n, _, dk = lt.shape
    nt = tlen // tb
    off = tok0 // tb
    lt4 = lt.reshape(n // tb, tb, KSEL, dk)
    v3 = vsrc.reshape(n // tb, tb, vsrc.shape[1])
    nsel = sel.shape[1]

    def tok_map(c, ncol):
        if ncol is None:
            return lambda bi, ti: (off + (bi * nb + c) * nt + ti, 0, 0, 0)
        return lambda bi, ti: (off + (bi * nb + c) * nt + ti, 0, ncol)

    in_specs = [pl.BlockSpec((None, tb, KSEL, dk), tok_map(c, None)) for c in range(nb)]
    in_specs += [pl.BlockSpec((None, tb, GW), tok_map(c, vcol)) for c in range(nb)]
    in_specs += [pl.BlockSpec((KSEL, nsel), lambda bi, ti: (0, 0)),
                 pl.BlockSpec((nb, dk, GW), lambda bi, ti: (bi, 0, 0))]
    out_specs = [pl.BlockSpec((nb, tb, GW), lambda bi, ti: (bi, ti, 0)),
                 pl.BlockSpec((nb, dk, GW), lambda bi, ti: (bi, 0, 0))]
    return pl.pallas_call(
        functools.partial(_rec_kernel, mode=mode, nb=nb, tb=tb),
        grid=(bsz // nb, nt), in_specs=in_specs, out_specs=out_specs,
        out_shape=[jax.ShapeDtypeStruct((bsz, tlen, GW), F32), jax.ShapeDtypeStruct((bsz, dk, GW), F32)],
        compiler_params=_cparams(("parallel", "arbitrary")), name="rec_" + mode,
    )(*([lt4] * nb), *([v3] * nb), sel, s0)


def _log1m_sigmoid(z):
    return jnp.minimum(-z, 0.0) - jnp.log1p(jnp.exp(-jnp.abs(z)))


def _sb_block(z, mask, u, carry, rowsum_lanes):
    l1m = _log1m_sigmoid(z)
    if mask is not None:
        l1m = jnp.where(mask, l1m, 0.0)
    hi, lo = _split2(l1m)
    between = carry + (_dot(hi, u) + _dot(lo, u))
    logw = z + l1m + between
    if mask is not None:
        logw = jnp.where(mask, logw, -jnp.inf)
    wgt = jnp.exp(logw)
    if rowsum_lanes is None:
        total = jnp.sum(l1m, axis=1, keepdims=True)
    else:
        total = _dot(hi, rowsum_lanes) + _dot(lo, rowsum_lanes)
    return wgt, carry + total


def _sbp_kernel(bias_ref, q_ref, k_ref, v_ref, u_ref, o_ref, *, tq):
    qi = pl.program_id(1)
    u = u_ref[...]
    nt_dims = (((1,), (1,)), ((), ()))
    rows = lax.broadcasted_iota(jnp.int32, (tq, tq), 0)
    cols = lax.broadcasted_iota(jnp.int32, (tq, tq), 1)
    lanes = [slice(h * HEAD_DIM, (h + 1) * HEAD_DIM) for h in range(HEADS)]
    qs = [(q_ref[:, lanes[h]] * (HEAD_DIM ** -0.5)).astype(BF16) for h in range(HEADS)]

    def block(j, state, mask):
        start = pl.multiple_of(j * tq, tq)
        out = []
        for h in range(HEADS):
            carry, acc = state[h]
            kb = k_ref[pl.ds(start, tq), lanes[h]].astype(BF16)
            vb = v_ref[pl.ds(start, tq), lanes[h]].astype(BF16)
            z = lax.dot_general(qs[h], kb, nt_dims, preferred_element_type=F32) + bias_ref[h]
            wgt, carry = _sb_block(z, mask, u, carry, None)
            out.append((carry, acc + _dot(wgt.astype(BF16), vb)))
        return tuple(out)

    zero = (jnp.zeros((tq, 1), F32), jnp.zeros((tq, HEAD_DIM), F32))
    state = block(qi, (zero,) * HEADS, cols < rows)
    state = lax.fori_loop(0, qi, lambda i, st: block(qi - 1 - i, st, None), state)
    for h in range(HEADS):
        o_ref[:, lanes[h]] = state[h][1]


def _sb_prompt(proj_sb, bias, u, *, bsz, tlen, tq):
    nq = tlen // tq
    grid_spec = pltpu.PrefetchScalarGridSpec(
        num_scalar_prefetch=1, grid=(bsz, nq),
        in_specs=[pl.BlockSpec((tq, GW), lambda b, qi, bias: (b * nq + qi, 0)),
                  pl.BlockSpec((tlen, GW), lambda b, qi, bias: (b, 1)),
                  pl.BlockSpec((tlen, GW), lambda b, qi, bias: (b, 2)),
                  pl.BlockSpec((tq, tq), lambda b, qi, bias: (0, 0))],
        out_specs=pl.BlockSpec((tq, GW), lambda b, qi, bias: (b * nq + qi, 0)))
    return pl.pallas_call(
        functools.partial(_sbp_kernel, tq=tq), grid_spec=grid_spec,
        out_shape=jax.ShapeDtypeStruct((bsz * tlen, GW), F32),
        compiler_params=_cparams(("parallel", "arbitrary")), name="sb_prompt",
    )(bias, proj_sb, proj_sb, proj_sb, u)


SBS_PAGES = 8


def _sbs_kernel(pt_ref, bias_ref, q_ref, kn_ref, vn_ref, *rest, tnew):
    kp_refs = rest[0:SBS_PAGES]
    vp_refs = rest[SBS_PAGES:2 * SBS_PAGES]
    uo_ref, o_ref, acc_ref, carry_ref, knew_ref, vnew_ref = rest[2 * SBS_PAGES:]
    s = pl.program_id(1)
    rows_n = HEADS * tnew
    uo = uo_ref[...]
    nt_dims = (((1,), (1,)), ((), ()))

    row_head = lax.broadcasted_iota(jnp.int32, (rows_n, GW), 0) // tnew
    lane_head = lax.broadcasted_iota(jnp.int32, (rows_n, GW), 1) // HEAD_DIM
    q = q_ref[...] * (HEAD_DIM ** -0.5)
    qbd = jnp.where(row_head == lane_head, jnp.concatenate([q] * HEADS, axis=0), 0.0).astype(BF16)
    rh = lax.broadcasted_iota(jnp.int32, (rows_n, PAGE), 0) // tnew
    bias = jnp.zeros((rows_n, PAGE), F32)
    for hh in range(HEADS):
        bias = jnp.where(rh == hh, bias_ref[hh], bias)

    def local(kb, mask):
        z = lax.dot_general(qbd, kb.astype(BF16), nt_dims, preferred_element_type=F32) + bias
        l1m = _log1m_sigmoid(z)
        if mask is not None:
            l1m = jnp.where(mask, l1m, 0.0)
        su = _dot_exact_rhs(l1m, uo)
        return z + l1m + su[:, 0:PAGE], su[:, PAGE:2 * PAGE]

    @pl.when(s == 0)
    def _():
        knew_ref[...] = jnp.zeros_like(knew_ref)
        vnew_ref[...] = jnp.zeros_like(vnew_ref)
        knew_ref[0:tnew, :] = kn_ref[...]
        vnew_ref[0:tnew, :] = vn_ref[...]
        trow = lax.broadcasted_iota(jnp.int32, (rows_n, PAGE), 0) % tnew
        col = lax.broadcasted_iota(jnp.int32, (rows_n, PAGE), 1)
        mask = col < trow
        logw, tot = local(knew_ref[...], mask)
        wgt = jnp.exp(jnp.where(mask, logw, -jnp.inf))
        acc_ref[...] = _dot(wgt.astype(BF16), vnew_ref[...].astype(BF16))
        carry_ref[...] = tot

    parts = [local(kp_refs[c][...], None) for c in range(SBS_PAGES)]
    carry = carry_ref[...]
    acc = acc_ref[...]
    for c in range(SBS_PAGES):
        logw, tot = parts[c]
        acc += _dot(jnp.exp(logw + carry).astype(BF16), vp_refs[c][...].astype(BF16))
        carry = carry + tot
    carry_ref[...] = carry
    acc_ref[...] = acc

    @pl.when(s == pl.num_programs(1) - 1)
    def _():
        acc = acc_ref[...]
        lh = lax.broadcasted_iota(jnp.int32, (tnew, GW), 1) // HEAD_DIM
        out = jnp.zeros((tnew, GW), F32)
        for hh in range(HEADS):
            out = jnp.where(lh == hh, acc[hh * tnew:(hh + 1) * tnew, :], out)
        o_ref[...] = out


def _sb_sample(proj_sb, cache_k, cache_v, layer, page_table, bias, uo, *, tok0, bsz, tnew):
    n = proj_sb.shape[0]
    npages = page_table.shape[1]
    nsteps = npages // SBS_PAGES
    p3 = proj_sb.reshape(n // tnew, tnew, SB_W)
    off = tok0 // tnew
    tok = lambda col: (lambda b, s, pt, bias: (off + b, 0, col))

    def page(c):
        return lambda b, s, pt, bias: (layer, pt[b, npages - 1 - (s * SBS_PAGES + c)], 0, 0)

    fixed = lambda b, s, pt, bias: (0, 0)
    in_specs = [pl.BlockSpec((None, tnew, GW), tok(0)), pl.BlockSpec((None, tnew, GW), tok(1)),
                pl.BlockSpec((None, tnew, GW), tok(2))]
    in_specs += [pl.BlockSpec((None, None, PAGE, GW), page(c)) for c in range(SBS_PAGES)]
    in_specs += [pl.BlockSpec((None, None, PAGE, GW), page(c)) for c in range(SBS_PAGES)]
    in_specs += [pl.BlockSpec((PAGE, 2 * PAGE), fixed)]
    grid_spec = pltpu.PrefetchScalarGridSpec(
        num_scalar_prefetch=2, grid=(bsz, nsteps), in_specs=in_specs,
        out_specs=pl.BlockSpec((None, tnew, GW), lambda b, s, pt, bias: (b, 0, 0)),
        scratch_shapes=[pltpu.VMEM((HEADS * tnew, GW), F32), pltpu.VMEM((HEADS * tnew, PAGE), F32),
                        pltpu.VMEM((PAGE, GW), F32), pltpu.VMEM((PAGE, GW), F32)])
    return pl.pallas_call(
        functools.partial(_sbs_kernel, tnew=tnew), grid_spec=grid_spec,
        out_shape=jax.ShapeDtypeStruct((bsz, tnew, GW), F32),
        compiler_params=_cparams(("parallel", "arbitrary")), name="sb_sample",
    )(page_table, bias, p3, p3, p3, *([cache_k] * SBS_PAGES), *([cache_v] * SBS_PAGES), uo)


def _out_kernel(x_ref, yhg_ref, hgg_ref, yrw_ref, rwrkv_ref, rwg_ref, osb_ref, ygla_ref, glag_ref,
                e_ref, hgn_ref, lnw_ref, lnb_ref, glan_ref, post_ref, w_ref, o_ref):
    e = e_ref[...]
    inv = 1.0 / HEAD_DIM

    def head_rms(y, g):
        return y * lax.rsqrt(_dot_exact_rhs(y * y, e) * inv + RMS_EPS) * g

    def silu(t):
        return t * _sigmoid(t)

    o_hg = head_rms(yhg_ref[...], hgn_ref[...]) * silu(hgg_ref[...])
    y = yrw_ref[...]
    mean = _dot_exact_rhs(y, e) * inv
    yc = y - mean
    var = _dot_exact_rhs(yc * yc, e) * inv
    o_rw = (yc * lax.rsqrt(var + RW_GN_EPS) * lnw_ref[...] + lnb_ref[...] + rwrkv_ref[...]) * rwg_ref[...]
    o_gla = head_rms(ygla_ref[...], glan_ref[...]) * silu(glag_ref[...])
    acc = _dot(o_hg.astype(BF16), w_ref[0:GW, :])
    acc += _dot(o_rw.astype(BF16), w_ref[GW:2 * GW, :])
    acc += _dot(osb_ref[...].astype(BF16), w_ref[2 * GW:3 * GW, :])
    acc += _dot(o_gla.astype(BF16), w_ref[3 * GW:4 * GW, :])
    o_ref[...] = x_ref[...] + _rms_rows(acc, post_ref[...])


def _out_proj(x, y_hg, proj_hg, y_rw, rw_rkv, rw_g, o_sb, y_gla, proj_gla, e256, P, tm):
    n = x.shape[0]
    row = lambda i: (i, 0)
    fixed = lambda i: (0, 0)
    blk = pl.BlockSpec((tm, GW), row)
    vec = pl.BlockSpec((1, GW), fixed)
    in_specs = [pl.BlockSpec((tm, D_MODEL), row), blk, pl.BlockSpec((tm, GW), lambda i: (i, 3)),
                blk, blk, blk, blk, blk, pl.BlockSpec((tm, GW), lambda i: (i, 2)),
                pl.BlockSpec((GW, GW), fixed), vec, vec, vec, vec,
                pl.BlockSpec((1, D_MODEL), fixed), pl.BlockSpec((D_MODEL, D_MODEL), fixed)]
    return pl.pallas_call(
        _out_kernel, grid=(n // tm,), in_specs=in_specs, out_specs=pl.BlockSpec((tm, D_MODEL), row),
        out_shape=jax.ShapeDtypeStruct((n, D_MODEL), F32),
        compiler_params=_cparams(("parallel",)), name="out_proj",
    )(x, y_hg, proj_hg, y_rw, rw_rkv, rw_g, o_sb, y_gla, proj_gla, e256,
      P['hg_norm'], P['rw_ln_w'], P['rw_ln_b'], P['gla_norm'], P['norm_mix_post'], P['w_out'])


def _ffn_kernel(x_ref, gpre_ref, gpost_ref, wu_ref, wd_ref, o_ref, h_ref, acc_ref):
    f = pl.program_id(1)

    @pl.when(f == 0)
    def _():
        h_ref[...] = _rms_rows(x_ref[...], gpre_ref[...]).astype(BF16)
        acc_ref[...] = jnp.zeros_like(acc_ref)

    u = jnp.maximum(_dot(h_ref[...], wu_ref[...]), 0.0)
    acc_ref[...] += _dot((u * u).astype(BF16), wd_ref[...])

    @pl.when(f == pl.num_programs(1) - 1)
    def _():
        o_ref[...] = x_ref[...] + _rms_rows(acc_ref[...], gpost_ref[...])


def _ffn(x, gpre, gpost, wu, wd, tm, tf):
    n = x.shape[0]
    return pl.pallas_call(
        _ffn_kernel, grid=(n // tm, D_FF // tf),
        in_specs=[pl.BlockSpec((tm, D_MODEL), lambda i, f: (i, 0)),
                  pl.BlockSpec((1, D_MODEL), lambda i, f: (0, 0)), pl.BlockSpec((1, D_MODEL), lambda i, f: (0, 0)),
                  pl.BlockSpec((D_MODEL, tf), lambda i, f: (0, f)), pl.BlockSpec((tf, D_MODEL), lambda i, f: (f, 0))],
        out_specs=pl.BlockSpec((tm, D_MODEL), lambda i, f: (i, 0)),
        out_shape=jax.ShapeDtypeStruct((n, D_MODEL), F32),
        scratch_shapes=[pltpu.VMEM((tm, D_MODEL), BF16), pltpu.VMEM((tm, D_MODEL), F32)],
        compiler_params=_cparams(("parallel", "arbitrary")), name="ffn",
    )(x, gpre, gpost, wu, wd)


def _selector(nvec):
    k = np.arange(KSEL)
    col = np.arange(nvec * GW)
    vk, hk = k // (2 * HEADS), k % HEADS
    sel = (vk[:, None] == (col // GW)[None, :]) & (hk[:, None] == ((col % GW) // HEAD_DIM)[None, :])
    sel &= (k < nvec * 2 * HEADS)[:, None]
    return jnp.asarray(sel, BF16)


def _pad_rows(w, r0, rows):
    return jnp.zeros((rows, w.shape[1]), F32).at[r0:r0 + w.shape[0]].set(w.astype(F32))


def _layer_params(l, norm_mix_pre, norm_mix_post, norm_ffn_pre, norm_ffn_post, w_in, w_out, w_up, w_down,
                  hg_norm, rw_mu, rw_w0, rw_w2, rw_a0, rw_a2, rw_g2, rw_k_k, rw_k_a, rw_r_k, rw_ln_w, rw_ln_b,
                  sb_bias, gla_a2, gla_a_b, gla_norm):
    r = lambda a: a[l].astype(F32).reshape(1, -1)
    w = w_in[l]
    gla0 = HG_W + RW_PROJ + SB_W
    w_pad = jnp.concatenate(
        [w[:, :gla0 + 512], w[:, gla0 + 512 + GLA_RANK:], w[:, gla0 + 512:gla0 + 512 + GLA_RANK],
         jnp.zeros((D_MODEL, LANES - GLA_RANK), w.dtype)], axis=1).astype(BF16)
    return {
        'norm_mix_pre': r(norm_mix_pre), 'norm_mix_post': r(norm_mix_post),
        'norm_ffn_pre': r(norm_ffn_pre), 'norm_ffn_post': r(norm_ffn_post),
        'w_in': w_pad, 'w_out': w_out[l].astype(BF16), 'w_up': w_up[l].astype(BF16), 'w_down': w_down[l].astype(BF16),
        'hg_norm': jnp.tile(r(hg_norm), (1, HEADS)), 'gla_norm': jnp.tile(r(gla_norm), (1, HEADS)),
        'rw_mu': r(rw_mu), 'rw_w0': r(rw_w0), 'rw_a0': r(rw_a0), 'rw_k_k': r(rw_k_k), 'rw_k_a': r(rw_k_a),
        'rw_r_k': r(rw_r_k), 'rw_ln_w': r(rw_ln_w), 'rw_ln_b': r(rw_ln_b),
        'rw_w2p': _pad_rows(rw_w2[l], 0, LANES), 'rw_a2p': _pad_rows(rw_a2[l], 32, LANES),
        'rw_g2p': _pad_rows(rw_g2[l], 64, LANES),
        'gla_a2p': _pad_rows(gla_a2[l], 0, LANES), 'gla_a_b': r(gla_a_b),
        'sb_bias': sb_bias[l].astype(F32),
    }


def _tile(n, pref):
    t = pref
    while n % t:
        t //= 2
    return t


def _to_lhs(src, dk):
    return src.reshape(src.shape[0], KSEL, dk)


def _state_in(s):
    b, h, dk, dv = s.shape
    return jnp.transpose(s.astype(F32), (0, 2, 1, 3)).reshape(b, dk, h * dv)


def _state_out(s):
    b, dk, _ = s.shape
    return jnp.transpose(s.reshape(b, dk, HEADS, HEAD_DIM), (0, 2, 1, 3))


def kernel(x_prompt, x_sample, cache_sb_k, cache_sb_v, page_table, state_hgrn, state_rwkv, state_rwkv_shift,
           state_gla, norm_mix_pre, norm_mix_post, norm_ffn_pre, norm_ffn_post, w_in, w_out, w_up, w_down,
           hg_lb, hg_norm, rw_mu, rw_w0, rw_w2, rw_a0, rw_a2, rw_g2, rw_k_k, rw_k_a, rw_r_k, rw_ln_w, rw_ln_b,
           sb_bias, gla_a2, gla_a_b, gla_norm):
    depth = w_in.shape[0]
    bp, tp, _ = x_prompt.shape
    bs, ts, _ = x_sample.shape
    n_p, n_s = bp * tp, bs * ts
    n = n_p + n_s
    n_pool = cache_sb_k.shape[1]
    cache_k = cache_sb_k.reshape(depth, n_pool, PAGE, GW)
    cache_v = cache_sb_v.reshape(depth, n_pool, PAGE, GW)

    lb_all = jnp.cumsum(jax.nn.softmax(hg_lb.astype(F32), axis=0), axis=0)
    lb_all = lb_all - lb_all[0:1]
    e256 = jnp.asarray(np.arange(GW)[:, None] // HEAD_DIM == np.arange(GW)[None, :] // HEAD_DIM, BF16)
    sel5, sel3 = _selector(5), _selector(3)
    tq = min(256, tp)
    u_p = jnp.asarray(np.arange(tq)[:, None] > np.arange(tq)[None, :], BF16)
    uo_s = jnp.asarray(np.concatenate([np.arange(PAGE)[:, None] > np.arange(PAGE)[None, :],
                                       np.ones((PAGE, PAGE), bool)], axis=1), BF16)
    tm = _tile(n, 512)
    tm_prep = _tile(n, 256)
    tm_ffn = _tile(n, 1024)
    tb_p = min(128, tp)

    x = jnp.concatenate([x_prompt.reshape(n_p, D_MODEL), x_sample.reshape(n_s, D_MODEL)], axis=0)
    outs = {k: [] for k in ('sbk_p', 'sbv_p', 'sbk_s', 'sbv_s', 'hg_p', 'hg_s', 'rw_p', 'rw_s', 'sh_p', 'sh_s',
                            'gl_p', 'gl_s')}
    for l in range(depth):
        P = _layer_params(l, norm_mix_pre, norm_mix_post, norm_ffn_pre, norm_ffn_post, w_in, w_out, w_up, w_down,
                          hg_norm, rw_mu, rw_w0, rw_w2, rw_a0, rw_a2, rw_g2, rw_k_k, rw_k_a, rw_r_k, rw_ln_w,
                          rw_ln_b, sb_bias, gla_a2, gla_a_b, gla_norm)
        proj_hg, proj_rw, proj_sb, proj_gla = _proj(x, P['norm_mix_pre'], P['w_in'], tm)

        rw_s3 = proj_rw[n_p:].reshape(bs, ts, RW_PROJ)
        prev_s = jnp.concatenate([state_rwkv_shift[l].astype(F32)[:, None], rw_s3[:, :-1]], axis=1)
        rw_src, rw_v, rw_g, rw_rkv, hg_src, gla_src = _prep(
            proj_hg, proj_rw, prev_s.reshape(n_s, RW_PROJ), proj_gla, e256, lb_all[l].reshape(1, GW), P,
            tm_prep, n_p, tp)
        rw_lt, hg_lt, gla_lt = _to_lhs(rw_src, HEAD_DIM), _to_lhs(hg_src, HEAD_DIM), _to_lhs(gla_src, GLA_DK)

        def rec(lt, vsrc, vcol, sel, mode, dk, s0_sample):
            y_p, s_p = _recurrence(lt, vsrc, vcol, sel, jnp.zeros((bp, dk, GW), F32), mode=mode, tok0=0,
                                   bsz=bp, tlen=tp, nb=1, tb=tb_p)
            y_s, s_s = _recurrence(lt, vsrc, vcol, sel, s0_sample, mode=mode, tok0=n_p,
                                   bsz=bs, tlen=ts, nb=2, tb=ts)
            return jnp.concatenate([y_p.reshape(n_p, GW), y_s.reshape(n_s, GW)], axis=0), s_p, s_s

        y_hg, hgS_p, hgS_s = rec(hg_lt, proj_hg, 2, sel3, 'gla', HEAD_DIM, _state_in(state_hgrn[l]))
        rw_s0 = _state_in(jnp.swapaxes(state_rwkv[l], -1, -2))
        y_rw, rwS_p, rwS_s = rec(rw_lt, rw_v, 0, sel5, 'rwkv', HEAD_DIM, rw_s0)
        y_gla, glS_p, glS_s = rec(gla_lt, proj_gla, 1, sel3, 'gla', GLA_DK, _state_in(state_gla[l]))

        o_sb_p = _sb_prompt(proj_sb, P['sb_bias'], u_p, bsz=bp, tlen=tp, tq=tq)
        o_sb_s = _sb_sample(proj_sb, cache_k, cache_v, l, page_table, P['sb_bias'], uo_s,
                            tok0=n_p, bsz=bs, tnew=ts).reshape(n_s, GW)
        o_sb = jnp.concatenate([o_sb_p, o_sb_s], axis=0)

        x = _out_proj(x, y_hg, proj_hg, y_rw, rw_rkv, rw_g, o_sb, y_gla, proj_gla, e256, P, tm)
        x = _ffn(x, P['norm_ffn_pre'], P['norm_ffn_post'], P['w_up'], P['w_down'], tm_ffn, 1024)

        sb_p = proj_sb[:n_p].reshape(bp, tp, 3, HEADS, HEAD_DIM)
        sb_s = proj_sb[n_p:].reshape(bs, ts, 3, HEADS, HEAD_DIM)
        rw_p3 = proj_rw[:n_p].reshape(bp, tp, RW_PROJ)
        outs['sbk_p'].append(sb_p[:, :, 1]); outs['sbv_p'].append(sb_p[:, :, 2])
        outs['sbk_s'].append(sb_s[:, :, 1]); outs['sbv_s'].append(sb_s[:, :, 2])
        outs['hg_p'].append(_state_out(hgS_p)); outs['hg_s'].append(_state_out(hgS_s).astype(state_hgrn.dtype))
        outs['rw_p'].append(jnp.swapaxes(_state_out(rwS_p), -1, -2))
        outs['rw_s'].append(jnp.swapaxes(_state_out(rwS_s), -1, -2).astype(state_rwkv.dtype))
        outs['sh_p'].append(rw_p3[:, -1]); outs['sh_s'].append(rw_s3[:, -1].astype(state_rwkv_shift.dtype))
        outs['gl_p'].append(_state_out(glS_p)); outs['gl_s'].append(_state_out(glS_s).astype(state_gla.dtype))

    st = lambda k: jnp.stack(outs[k], 0)
    return (x[:n_p].reshape(bp, tp, D_MODEL), x[n_p:].reshape(bs, ts, D_MODEL),
            st('sbk_p'), st('sbv_p'), st('sbk_s'), st('sbv_s'), st('hg_p'), st('hg_s'),
            st('rw_p'), st('rw_s'), st('sh_p'), st('sh_s'), st('gl_p'), st('gl_s'))
```

```python
import functools

import jax
import jax.numpy as jnp
import numpy as np
from jax import lax
from jax.experimental import pallas as pl
from jax.experimental.pallas import tpu as pltpu

F32 = jnp.float32
BF16 = jnp.bfloat16

D_MODEL = 1024
HEADS = 4
HEAD_DIM = 64
GW = HEADS * HEAD_DIM
GLA_DK = 32
GLA_KW = HEADS * GLA_DK
GLA_RANK = 16
GLA_GATE_NORM = 16.0
RW_PROJ = 896
RW_GN_EPS = 64e-5
D_FF = 4096
RMS_EPS = 1e-6
PAGE = 128
LANES = 128
RW_PLANES = (16, 128)
HG_PLANES = (8, 128)
GLA_PLANES = (16, 32)
REC_TILE = 128
REC_UNROLL = 32
VMEM_LIMIT = 52 * 1024 * 1024

HG_W = 4 * GW
SB_W = 3 * GW
GLA_W = 896


def _cparams(sem):
    return pltpu.CompilerParams(dimension_semantics=sem, vmem_limit_bytes=VMEM_LIMIT)


def _split2(x):
    hi = x.astype(BF16)
    lo = (x - hi.astype(F32)).astype(BF16)
    return hi, lo


def _dot(a, b):
    return jnp.dot(a, b, preferred_element_type=F32)


def _dot_exact_rhs(x, e):
    hi, lo = _split2(x)
    return _dot(hi, e) + _dot(lo, e)


def _dot3(a, b):
    ah, al = _split2(a)
    bh, bl = _split2(b)
    return _dot(ah, bh) + (_dot(ah, bl) + _dot(al, bh))


def _sigmoid(x):
    return 1.0 / (1.0 + jnp.exp(-x))


def _softplus(x):
    return jnp.maximum(x, 0.0) + jnp.log1p(jnp.exp(-jnp.abs(x)))


def _rms_rows(x, g):
    return x * lax.rsqrt(jnp.mean(x * x, axis=-1, keepdims=True) + RMS_EPS) * g


def _proj_kernel(x_ref, g_ref, w_ref, hg_ref, rw_ref, sb_ref, gla_ref):
    h = _rms_rows(x_ref[...], g_ref[...]).astype(BF16)
    c0 = 0
    for ref, width in ((hg_ref, HG_W), (rw_ref, RW_PROJ), (sb_ref, SB_W), (gla_ref, GLA_W)):
        ref[...] = _dot(h, w_ref[:, c0:c0 + width])
        c0 += width


def _proj(x, g, w, tm):
    n = x.shape[0]
    wtot = HG_W + RW_PROJ + SB_W + GLA_W
    row = lambda i: (i, 0)
    fixed = lambda i: (0, 0)
    return pl.pallas_call(
        _proj_kernel,
        grid=(n // tm,),
        in_specs=[pl.BlockSpec((tm, D_MODEL), row), pl.BlockSpec((1, D_MODEL), fixed),
                  pl.BlockSpec((D_MODEL, wtot), fixed)],
        out_specs=[pl.BlockSpec((tm, HG_W), row), pl.BlockSpec((tm, RW_PROJ), row),
                   pl.BlockSpec((tm, SB_W), row), pl.BlockSpec((tm, GLA_W), row)],
        out_shape=[jax.ShapeDtypeStruct((n, HG_W), F32), jax.ShapeDtypeStruct((n, RW_PROJ), F32),
                   jax.ShapeDtypeStruct((n, SB_W), F32), jax.ShapeDtypeStruct((n, GLA_W), F32)],
        compiler_params=_cparams(("parallel",)),
        name="in_proj",
    )(x, g, w)


def _store_planes(ref, vecs):
    w = ref.shape[1]
    r = 0
    for x in vecs:
        for p in range(x.shape[1] // w):
            ref[r * REC_TILE:(r + 1) * REC_TILE, :] = x[:, p * w:(p + 1) * w]
            r += 1
    for rr in range(r, ref.shape[0] // REC_TILE):
        ref[rr * REC_TILE:(rr + 1) * REC_TILE, :] = jnp.zeros((REC_TILE, w), F32)


def _prep_kernel(hg_ref, rw_ref, tail_ref, prevs_ref, glaqk_ref, glaga_ref, e_ref, lb_ref, mu_ref, w0_ref, a0_ref,
                 kk_ref, ka_ref, rk_ref, w2_ref, a2_ref, g2_ref, ga2_ref, gab_ref,
                 rwsrc_ref, rwv_ref, rwg_ref, rwrkv_ref, hgsrc_ref, glasrc_ref, *, prompt_tiles, seq_tiles):
    e = e_ref[...]
    i = pl.program_id(0)
    hq = hg_ref[:, 0:GW]
    hf = hg_ref[:, GW:2 * GW]
    lb = lb_ref[...]
    f = lb + (1.0 - lb) * _sigmoid(hf)
    kf = (1.0 - lb) * _sigmoid(-hf)
    _store_planes(hgsrc_ref, (f, kf, hq * _sigmoid(hq)))

    p = rw_ref[...]
    row0 = jnp.where(i % seq_tiles == 0, 0.0, tail_ref[7:8, :])
    first = lax.broadcasted_iota(jnp.int32, p.shape, 0) == 0
    prev = jnp.where(first, row0, pltpu.roll(p, 1, axis=0))
    prev = jnp.where(i >= prompt_tiles, prevs_ref[...], prev)
    xs = p + (prev - p) * mu_ref[...]
    r = xs[:, 0:GW]
    k = xs[:, GW:2 * GW]
    v = xs[:, 2 * GW:3 * GW]
    lora = xs[:, 3 * GW:RW_PROJ]
    w = -_softplus(-(w0_ref[...] + _dot3(jnp.tanh(lora), w2_ref[...]))) - 0.5
    decay = jnp.exp(-jnp.exp(w))
    a = _sigmoid(a0_ref[...] + _dot3(lora, a2_ref[...]))
    g = _dot3(_sigmoid(lora), g2_ref[...])
    kk = k * kk_ref[...]
    kk = kk / jnp.maximum(jnp.sqrt(_dot_exact_rhs(kk * kk, e)), 1e-12)
    kp = k * (1.0 + (a - 1.0) * ka_ref[...])
    rwv_ref[...] = v
    rwg_ref[...] = g
    rwrkv_ref[...] = _dot_exact_rhs(r * kp * rk_ref[...], e) * v
    _store_planes(rwsrc_ref, (kk, decay, kk * a, kp, r))

    gq = glaqk_ref[:, 0:GLA_KW]
    gk = glaqk_ref[:, GLA_KW:2 * GLA_KW]
    xg = _dot3(glaga_ref[...], ga2_ref[...]) + gab_ref[...]
    dec = jnp.exp(-_softplus(-xg) / GLA_GATE_NORM)
    _store_planes(glasrc_ref, (dec, gk, gq * (GLA_DK ** -0.5)))


def _prep(proj_hg, proj_rw, prev_s, proj_gla, e256, lb, P, tm, n_p, tp):
    n = proj_hg.shape[0]
    assert tp % tm == 0 and n_p % tm == 0 and prev_s.shape[0] % tm == 0 and tm % 8 == 0
    prompt_tiles = n_p // tm
    row = lambda i: (i, 0)
    fixed = lambda i: (0, 0)
    vec = lambda wd: pl.BlockSpec((1, wd), fixed)
    in_specs = [
        pl.BlockSpec((tm, 2 * GW), row),
        pl.BlockSpec((tm, RW_PROJ), row),
        pl.BlockSpec((8, RW_PROJ), lambda i: (jnp.maximum(i * (tm // 8) - 1, 0), 0)),
        pl.BlockSpec((tm, RW_PROJ), lambda i: (jnp.maximum(i - prompt_tiles, 0), 0)),
        pl.BlockSpec((tm, 2 * GLA_KW), row),
        pl.BlockSpec((tm, LANES), lambda i: (i, 6)),
        pl.BlockSpec((GW, GW), fixed),
        vec(GW), vec(RW_PROJ), vec(GW), vec(GW), vec(GW), vec(GW), vec(GW),
        pl.BlockSpec((LANES, GW), fixed), pl.BlockSpec((LANES, GW), fixed), pl.BlockSpec((LANES, GW), fixed),
        pl.BlockSpec((LANES, GLA_KW), fixed), vec(GLA_KW),
    ]
    assert tm == REC_TILE
    plane = lambda shape: (jax.ShapeDtypeStruct((n // tm, shape[0] * tm, shape[1]), F32),
                           pl.BlockSpec((None, shape[0] * tm, shape[1]), lambda i: (i, 0, 0)))
    flat = (jax.ShapeDtypeStruct((n, GW), F32), pl.BlockSpec((tm, GW), row))
    outs = [plane(RW_PLANES), flat, flat, flat, plane(HG_PLANES), plane(GLA_PLANES)]
    out_shapes = [o[0] for o in outs]
    out_specs = [o[1] for o in outs]
    return pl.pallas_call(
        functools.partial(_prep_kernel, prompt_tiles=prompt_tiles, seq_tiles=tp // tm),
        grid=(n // tm,), in_specs=in_specs, out_specs=out_specs, out_shape=out_shapes,
        compiler_params=_cparams(("parallel",)), name="mixer_prep",
    )(proj_hg, proj_rw, proj_rw, prev_s, proj_gla, proj_gla, e256, lb, P['rw_mu'], P['rw_w0'], P['rw_a0'],
      P['rw_k_k'], P['rw_k_a'], P['rw_r_k'], P['rw_w2p'], P['rw_a2p'], P['rw_g2p'], P['gla_a2p'], P['gla_a_b'])


def _rec_token(mode, dk, lt_ref, v_ref, sel_ref, y_ref, row, s):
    rows = lt_ref.shape[0] // REC_TILE
    planes = lt_ref[pl.ds(row, rows, stride=REC_TILE), :]
    groups = planes.shape[1] // dk
    x = planes if groups == 1 else jnp.concatenate([planes[:, o * dk:(o + 1) * dk] for o in range(groups)], axis=0)
    hi, lo = _split2(x)
    lhs = jnp.concatenate([hi, lo], axis=0)
    z = lax.dot_general(lhs, sel_ref[...], (((0,), (0,)), ((), ())), preferred_element_type=F32)
    v = v_ref[pl.ds(row, 1), :]
    if mode == 'rwkv':
        kk, w, kka, k, r = (z[:, i * GW:(i + 1) * GW] for i in range(5))
        sa = -jnp.sum(s * kk, axis=0, keepdims=True)
        s = s * w + kka * sa + k * v
        y = jnp.sum(s * r, axis=0, keepdims=True)
    else:
        g, k, q = (z[:, i * GW:(i + 1) * GW] for i in range(3))
        s = s * g + k * v
        y = jnp.sum(s * q, axis=0, keepdims=True)
    y_ref[pl.ds(row, 1), :] = y
    return s


def _rec_long_kernel(lt_ref, v_ref, sel_ref, y_ref, s_ref, *, mode, dk):
    @pl.when(pl.program_id(1) == 0)
    def _():
        s_ref[...] = jnp.zeros_like(s_ref)

    def body(g, s):
        for u in range(REC_UNROLL):
            s = _rec_token(mode, dk, lt_ref, v_ref, sel_ref, y_ref, g * REC_UNROLL + u, s)
        return s

    s_ref[0] = lax.fori_loop(0, REC_TILE // REC_UNROLL, body, s_ref[0])


def _rec_short_kernel(lt_ref, v_ref, sel_ref, s0_ref, y_ref, s_ref, *, mode, dk, tlen):
    def body(q, _):
        for c in range(2):
            seq = 2 * q + c
            s = s0_ref[seq]
            for u in range(tlen):
                s = _rec_token(mode, dk, lt_ref, v_ref, sel_ref, y_ref, seq * tlen + u, s)
            s_ref[seq] = s
        return 0

    lax.fori_loop(0, REC_TILE // tlen // 2, body, 0)


def _recurrence(planes, vsrc, vcol, sel, s0, *, mode, dk, tok0, bsz, tlen):
    _, prow, pw = planes.shape
    off = tok0 // REC_TILE
    out_shape = [jax.ShapeDtypeStruct((bsz * tlen, GW), F32), jax.ShapeDtypeStruct((bsz, dk, GW), F32)]
    if s0 is None:
        nt = tlen // REC_TILE
        blk = lambda b, ti: off + b * nt + ti
        return pl.pallas_call(
            functools.partial(_rec_long_kernel, mode=mode, dk=dk), grid=(bsz, nt),
            in_specs=[pl.BlockSpec((None, prow, pw), lambda b, ti: (blk(b, ti), 0, 0)),
                      pl.BlockSpec((REC_TILE, GW), lambda b, ti: (blk(b, ti), vcol)),
                      pl.BlockSpec(sel.shape, lambda b, ti: (0, 0))],
            out_specs=[pl.BlockSpec((REC_TILE, GW), lambda b, ti: (b * nt + ti, 0)),
                       pl.BlockSpec((1, dk, GW), lambda b, ti: (b, 0, 0))],
            out_shape=out_shape, compiler_params=_cparams(("parallel", "arbitrary")), name="rec_long_" + mode,
        )(planes, vsrc, sel)
    nseq = REC_TILE // tlen
    return pl.pallas_call(
        functools.partial(_rec_short_kernel, mode=mode, dk=dk, tlen=tlen), grid=(bsz // nseq,),
        in_specs=[pl.BlockSpec((None, prow, pw), lambda i: (off + i, 0, 0)),
                  pl.BlockSpec((REC_TILE, GW), lambda i: (off + i, vcol)),
                  pl.BlockSpec(sel.shape, lambda i: (0, 0)),
                  pl.BlockSpec((nseq, dk, GW), lambda i: (i, 0, 0))],
        out_specs=[pl.BlockSpec((REC_TILE, GW), lambda i: (i, 0)),
                   pl.BlockSpec((nseq, dk, GW), lambda i: (i, 0, 0))],
        out_shape=out_shape, compiler_params=_cparams(("parallel",)), name="rec_short_" + mode,
    )(planes, vsrc, sel, s0)


def _log1m_sigmoid(z):
    return jnp.minimum(-z, 0.0) - jnp.log1p(jnp.exp(-jnp.abs(z)))


def _sb_block(z, mask, u, carry, rowsum_lanes):
    l1m = _log1m_sigmoid(z)
    if mask is not None:
        l1m = jnp.where(mask, l1m, 0.0)
    hi, lo = _split2(l1m)
    between = carry + (_dot(hi, u) + _dot(lo, u))
    logw = z + l1m + between
    if mask is not None:
        logw = jnp.where(mask, logw, -jnp.inf)
    wgt = jnp.exp(logw)
    if rowsum_lanes is None:
        total = jnp.sum(l1m, axis=1, keepdims=True)
    else:
        total = _dot(hi, rowsum_lanes) + _dot(lo, rowsum_lanes)
    return wgt, carry + total


def _sbp_kernel(bias_ref, q_ref, k_ref, v_ref, u_ref, o_ref, *, tq):
    qi = pl.program_id(1)
    u = u_ref[...]
    nt_dims = (((1,), (1,)), ((), ()))
    rows = lax.broadcasted_iota(jnp.int32, (tq, tq), 0)
    cols = lax.broadcasted_iota(jnp.int32, (tq, tq), 1)
    lanes = [slice(h * HEAD_DIM, (h + 1) * HEAD_DIM) for h in range(HEADS)]
    qs = [(q_ref[:, lanes[h]] * (HEAD_DIM ** -0.5)).astype(BF16) for h in range(HEADS)]

    def block(j, state, mask):
        start = pl.multiple_of(j * tq, tq)
        out = []
        for h in range(HEADS):
            carry, acc = state[h]
            kb = k_ref[pl.ds(start, tq), lanes[h]].astype(BF16)
            vb = v_ref[pl.ds(start, tq), lanes[h]].astype(BF16)
            z = lax.dot_general(qs[h], kb, nt_dims, preferred_element_type=F32) + bias_ref[h]
            wgt, carry = _sb_block(z, mask, u, carry, None)
            out.append((carry, acc + _dot(wgt.astype(BF16), vb)))
        return tuple(out)

    zero = (jnp.zeros((tq, 1), F32), jnp.zeros((tq, HEAD_DIM), F32))
    state = block(qi, (zero,) * HEADS, cols < rows)
    state = lax.fori_loop(0, qi, lambda i, st: block(qi - 1 - i, st, None), state)
    for h in range(HEADS):
        o_ref[:, lanes[h]] = state[h][1]


def _sb_prompt(proj_sb, bias, u, *, bsz, tlen, tq):
    nq = tlen // tq
    grid_spec = pltpu.PrefetchScalarGridSpec(
        num_scalar_prefetch=1, grid=(bsz, nq),
        in_specs=[pl.BlockSpec((tq, GW), lambda b, qi, bias: (b * nq + qi, 0)),
                  pl.BlockSpec((tlen, GW), lambda b, qi, bias: (b, 1)),
                  pl.BlockSpec((tlen, GW), lambda b, qi, bias: (b, 2)),
                  pl.BlockSpec((tq, tq), lambda b, qi, bias: (0, 0))],
        out_specs=pl.BlockSpec((tq, GW), lambda b, qi, bias: (b * nq + qi, 0)))
    return pl.pallas_call(
        functools.partial(_sbp_kernel, tq=tq), grid_spec=grid_spec,
        out_shape=jax.ShapeDtypeStruct((bsz * tlen, GW), F32),
        compiler_params=_cparams(("parallel", "arbitrary")), name="sb_prompt",
    )(bias, proj_sb, proj_sb, proj_sb, u)


SBS_PAGES = 8


def _sbs_kernel(pt_ref, bias_ref, q_ref, kn_ref, vn_ref, *rest, tnew):
    kp_refs = rest[0:SBS_PAGES]
    vp_refs = rest[SBS_PAGES:2 * SBS_PAGES]
    uo_ref, o_ref, acc_ref, carry_ref, knew_ref, vnew_ref = rest[2 * SBS_PAGES:]
    s = pl.program_id(1)
    rows_n = HEADS * tnew
    uo = uo_ref[...]
    nt_dims = (((1,), (1,)), ((), ()))

    row_head = lax.broadcasted_iota(jnp.int32, (rows_n, GW), 0) // tnew
    lane_head = lax.broadcasted_iota(jnp.int32, (rows_n, GW), 1) // HEAD_DIM
    q = q_ref[...] * (HEAD_DIM ** -0.5)
    qbd = jnp.where(row_head == lane_head, jnp.concatenate([q] * HEADS, axis=0), 0.0).astype(BF16)
    rh = lax.broadcasted_iota(jnp.int32, (rows_n, PAGE), 0) // tnew
    bias = jnp.zeros((rows_n, PAGE), F32)
    for hh in range(HEADS):
        bias = jnp.where(rh == hh, bias_ref[hh], bias)

    def local(kb, mask):
        z = lax.dot_general(qbd, kb.astype(BF16), nt_dims, preferred_element_type=F32) + bias
        l1m = _log1m_sigmoid(z)
        if mask is not None:
            l1m = jnp.where(mask, l1m, 0.0)
        su = _dot_exact_rhs(l1m, uo)
        return z + l1m + su[:, 0:PAGE], su[:, PAGE:2 * PAGE]

    @pl.when(s == 0)
    def _():
        knew_ref[...] = jnp.zeros_like(knew_ref)
        vnew_ref[...] = jnp.zeros_like(vnew_ref)
        knew_ref[0:tnew, :] = kn_ref[...]
        vnew_ref[0:tnew, :] = vn_ref[...]
        trow = lax.broadcasted_iota(jnp.int32, (rows_n, PAGE), 0) % tnew
        col = lax.broadcasted_iota(jnp.int32, (rows_n, PAGE), 1)
        mask = col < trow
        logw, tot = local(knew_ref[...], mask)
        wgt = jnp.exp(jnp.where(mask, logw, -jnp.inf))
        acc_ref[...] = _dot(wgt.astype(BF16), vnew_ref[...].astype(BF16))
        carry_ref[...] = tot

    parts = [local(kp_refs[c][...], None) for c in range(SBS_PAGES)]
    carry = carry_ref[...]
    acc = acc_ref[...]
    for c in range(SBS_PAGES):
        logw, tot = parts[c]
        acc += _dot(jnp.exp(logw + carry).astype(BF16), vp_refs[c][...].astype(BF16))
        carry = carry + tot
    carry_ref[...] = carry
    acc_ref[...] = acc

    @pl.when(s == pl.num_programs(1) - 1)
    def _():
        acc = acc_ref[...]
        lh = lax.broadcasted_iota(jnp.int32, (tnew, GW), 1) // HEAD_DIM
        out = jnp.zeros((tnew, GW), F32)
        for hh in range(HEADS):
            out = jnp.where(lh == hh, acc[hh * tnew:(hh + 1) * tnew, :], out)
        o_ref[...] = out


def _sb_sample(proj_sb, cache_k, cache_v, layer, page_table, bias, uo, *, tok0, bsz, tnew):
    n = proj_sb.shape[0]
    npages = page_table.shape[1]
    nsteps = npages // SBS_PAGES
    p3 = proj_sb.reshape(n // tnew, tnew, SB_W)
    off = tok0 // tnew
    tok = lambda col: (lambda b, s, pt, bias: (off + b, 0, col))

    def page(c):
        return lambda b, s, pt, bias: (layer, pt[b, npages - 1 - (s * SBS_PAGES + c)], 0, 0)

    fixed = lambda b, s, pt, bias: (0, 0)
    in_specs = [pl.BlockSpec((None, tnew, GW), tok(0)), pl.BlockSpec((None, tnew, GW), tok(1)),
                pl.BlockSpec((None, tnew, GW), tok(2))]
    in_specs += [pl.BlockSpec((None, None, PAGE, GW), page(c)) for c in range(SBS_PAGES)]
    in_specs += [pl.BlockSpec((None, None, PAGE, GW), page(c)) for c in range(SBS_PAGES)]
    in_specs += [pl.BlockSpec((PAGE, 2 * PAGE), fixed)]
    grid_spec = pltpu.PrefetchScalarGridSpec(
        num_scalar_prefetch=2, grid=(bsz, nsteps), in_specs=in_specs,
        out_specs=pl.BlockSpec((None, tnew, GW), lambda b, s, pt, bias: (b, 0, 0)),
        scratch_shapes=[pltpu.VMEM((HEADS * tnew, GW), F32), pltpu.VMEM((HEADS * tnew, PAGE), F32),
                        pltpu.VMEM((PAGE, GW), F32), pltpu.VMEM((PAGE, GW), F32)])
    return pl.pallas_call(
        functools.partial(_sbs_kernel, tnew=tnew), grid_spec=grid_spec,
        out_shape=jax.ShapeDtypeStruct((bsz, tnew, GW), F32),
        compiler_params=_cparams(("parallel", "arbitrary")), name="sb_sample",
    )(page_table, bias, p3, p3, p3, *([cache_k] * SBS_PAGES), *([cache_v] * SBS_PAGES), uo)


def _out_kernel(x_ref, yhg_ref, hgg_ref, yrw_ref, rwrkv_ref, rwg_ref, osb_ref, ygla_ref, glag_ref,
                e_ref, hgn_ref, lnw_ref, lnb_ref, glan_ref, post_ref, w_ref, o_ref):
    e = e_ref[...]
    inv = 1.0 / HEAD_DIM

    def head_rms(y, g):
        return y * lax.rsqrt(_dot_exact_rhs(y * y, e) * inv + RMS_EPS) * g

    def silu(t):
        return t * _sigmoid(t)

    o_hg = head_rms(yhg_ref[...], hgn_ref[...]) * silu(hgg_ref[...])
    y = yrw_ref[...]
    mean = _dot_exact_rhs(y, e) * inv
    yc = y - mean
    var = _dot_exact_rhs(yc * yc, e) * inv
    o_rw = (yc * lax.rsqrt(var + RW_GN_EPS) * lnw_ref[...] + lnb_ref[...] + rwrkv_ref[...]) * rwg_ref[...]
    o_gla = head_rms(ygla_ref[...], glan_ref[...]) * silu(glag_ref[...])
    acc = _dot(o_hg.astype(BF16), w_ref[0:GW, :])
    acc += _dot(o_rw.astype(BF16), w_ref[GW:2 * GW, :])
    acc += _dot(osb_ref[...].astype(BF16), w_ref[2 * GW:3 * GW, :])
    acc += _dot(o_gla.astype(BF16), w_ref[3 * GW:4 * GW, :])
    o_ref[...] = x_ref[...] + _rms_rows(acc, post_ref[...])


def _out_proj(x, y_hg, proj_hg, y_rw, rw_rkv, rw_g, o_sb, y_gla, proj_gla, e256, P, tm):
    n = x.shape[0]
    row = lambda i: (i, 0)
    fixed = lambda i: (0, 0)
    blk = pl.BlockSpec((tm, GW), row)
    vec = pl.BlockSpec((1, GW), fixed)
    in_specs = [pl.BlockSpec((tm, D_MODEL), row), blk, pl.BlockSpec((tm, GW), lambda i: (i, 3)),
                blk, blk, blk, blk, blk, pl.BlockSpec((tm, GW), lambda i: (i, 2)),
                pl.BlockSpec((GW, GW), fixed), vec, vec, vec, vec,
                pl.BlockSpec((1, D_MODEL), fixed), pl.BlockSpec((D_MODEL, D_MODEL), fixed)]
    return pl.pallas_call(
        _out_kernel, grid=(n // tm,), in_specs=in_specs, out_specs=pl.BlockSpec((tm, D_MODEL), row),
        out_shape=jax.ShapeDtypeStruct((n, D_MODEL), F32),
        compiler_params=_cparams(("parallel",)), name="out_proj",
    )(x, y_hg, proj_hg, y_rw, rw_rkv, rw_g, o_sb, y_gla, proj_gla, e256,
      P['hg_norm'], P['rw_ln_w'], P['rw_ln_b'], P['gla_norm'], P['norm_mix_post'], P['w_out'])


def _ffn_kernel(x_ref, gpre_ref, gpost_ref, wu_ref, wd_ref, o_ref, h_ref, acc_ref):
    f = pl.program_id(1)

    @pl.when(f == 0)
    def _():
        h_ref[...] = _rms_rows(x_ref[...], gpre_ref[...]).astype(BF16)
        acc_ref[...] = jnp.zeros_like(acc_ref)

    u = jnp.maximum(_dot(h_ref[...], wu_ref[...]), 0.0)
    acc_ref[...] += _dot((u * u).astype(BF16), wd_ref[...])

    @pl.when(f == pl.num_programs(1) - 1)
    def _():
        o_ref[...] = x_ref[...] + _rms_rows(acc_ref[...], gpost_ref[...])


def _ffn(x, gpre, gpost, wu, wd, tm, tf):
    n = x.shape[0]
    return pl.pallas_call(
        _ffn_kernel, grid=(n // tm, D_FF // tf),
        in_specs=[pl.BlockSpec((tm, D_MODEL), lambda i, f: (i, 0)),
                  pl.BlockSpec((1, D_MODEL), lambda i, f: (0, 0)), pl.BlockSpec((1, D_MODEL), lambda i, f: (0, 0)),
                  pl.BlockSpec((D_MODEL, tf), lambda i, f: (0, f)), pl.BlockSpec((tf, D_MODEL), lambda i, f: (f, 0))],
        out_specs=pl.BlockSpec((tm, D_MODEL), lambda i, f: (i, 0)),
        out_shape=jax.ShapeDtypeStruct((n, D_MODEL), F32),
        scratch_shapes=[pltpu.VMEM((tm, D_MODEL), BF16), pltpu.VMEM((tm, D_MODEL), F32)],
        compiler_params=_cparams(("parallel", "arbitrary")), name="ffn",
    )(x, gpre, gpost, wu, wd)


def _selector(nvec, plane_shape, dk):
    rows, width = plane_shape
    groups = width // dk
    k = np.arange(2 * groups * rows)
    grp, r = (k // rows) % groups, k % rows
    per_vec = HEADS * dk // width
    vec, head = r // per_vec, (r % per_vec) * groups + grp
    col = np.arange(nvec * GW)
    sel = (vec[:, None] == (col // GW)[None, :]) & (head[:, None] == ((col % GW) // HEAD_DIM)[None, :])
    return jnp.asarray(sel, BF16)


def _pad_rows(w, r0, rows):
    return jnp.zeros((rows, w.shape[1]), F32).at[r0:r0 + w.shape[0]].set(w.astype(F32))


def _layer_params(l, norm_mix_pre, norm_mix_post, norm_ffn_pre, norm_ffn_post, w_in, w_out, w_up, w_down,
                  hg_norm, rw_mu, rw_w0, rw_w2, rw_a0, rw_a2, rw_g2, rw_k_k, rw_k_a, rw_r_k, rw_ln_w, rw_ln_b,
                  sb_bias, gla_a2, gla_a_b, gla_norm):
    r = lambda a: a[l].astype(F32).reshape(1, -1)
    w = w_in[l]
    gla0 = HG_W + RW_PROJ + SB_W
    w_pad = jnp.concatenate(
        [w[:, :gla0 + 512], w[:, gla0 + 512 + GLA_RANK:], w[:, gla0 + 512:gla0 + 512 + GLA_RANK],
         jnp.zeros((D_MODEL, LANES - GLA_RANK), w.dtype)], axis=1).astype(BF16)
    return {
        'norm_mix_pre': r(norm_mix_pre), 'norm_mix_post': r(norm_mix_post),
        'norm_ffn_pre': r(norm_ffn_pre), 'norm_ffn_post': r(norm_ffn_post),
        'w_in': w_pad, 'w_out': w_out[l].astype(BF16), 'w_up': w_up[l].astype(BF16), 'w_down': w_down[l].astype(BF16),
        'hg_norm': jnp.tile(r(hg_norm), (1, HEADS)), 'gla_norm': jnp.tile(r(gla_norm), (1, HEADS)),
        'rw_mu': r(rw_mu), 'rw_w0': r(rw_w0), 'rw_a0': r(rw_a0), 'rw_k_k': r(rw_k_k), 'rw_k_a': r(rw_k_a),
        'rw_r_k': r(rw_r_k), 'rw_ln_w': r(rw_ln_w), 'rw_ln_b': r(rw_ln_b),
        'rw_w2p': _pad_rows(rw_w2[l], 0, LANES), 'rw_a2p': _pad_rows(rw_a2[l], 32, LANES),
        'rw_g2p': _pad_rows(rw_g2[l], 64, LANES),
        'gla_a2p': _pad_rows(gla_a2[l], 0, LANES), 'gla_a_b': r(gla_a_b),
        'sb_bias': sb_bias[l].astype(F32),
    }


def _tile(n, pref):
    t = pref
    while n % t:
        t //= 2
    return t


def _state_in(s):
    b, h, dk, dv = s.shape
    return jnp.transpose(s.astype(F32), (0, 2, 1, 3)).reshape(b, dk, h * dv)


def _state_out(s):
    b, dk, _ = s.shape
    return jnp.transpose(s.reshape(b, dk, HEADS, HEAD_DIM), (0, 2, 1, 3))


def kernel(x_prompt, x_sample, cache_sb_k, cache_sb_v, page_table, state_hgrn, state_rwkv, state_rwkv_shift,
           state_gla, norm_mix_pre, norm_mix_post, norm_ffn_pre, norm_ffn_post, w_in, w_out, w_up, w_down,
           hg_lb, hg_norm, rw_mu, rw_w0, rw_w2, rw_a0, rw_a2, rw_g2, rw_k_k, rw_k_a, rw_r_k, rw_ln_w, rw_ln_b,
           sb_bias, gla_a2, gla_a_b, gla_norm):
    depth = w_in.shape[0]
    bp, tp, _ = x_prompt.shape
    bs, ts, _ = x_sample.shape
    n_p, n_s = bp * tp, bs * ts
    n = n_p + n_s
    n_pool = cache_sb_k.shape[1]
    cache_k = cache_sb_k.reshape(depth, n_pool, PAGE, GW)
    cache_v = cache_sb_v.reshape(depth, n_pool, PAGE, GW)

    lb_all = jnp.cumsum(jax.nn.softmax(hg_lb.astype(F32), axis=0), axis=0)
    lb_all = lb_all - lb_all[0:1]
    e256 = jnp.asarray(np.arange(GW)[:, None] // HEAD_DIM == np.arange(GW)[None, :] // HEAD_DIM, BF16)
    sel_rw = _selector(5, RW_PLANES, HEAD_DIM)
    sel_hg = _selector(3, HG_PLANES, HEAD_DIM)
    sel_gla = _selector(3, GLA_PLANES, GLA_DK)
    tq = min(256, tp)
    u_p = jnp.asarray(np.arange(tq)[:, None] > np.arange(tq)[None, :], BF16)
    uo_s = jnp.asarray(np.concatenate([np.arange(PAGE)[:, None] > np.arange(PAGE)[None, :],
                                       np.ones((PAGE, PAGE), bool)], axis=1), BF16)
    tm = _tile(n, 512)
    tm_prep = REC_TILE
    tm_ffn = _tile(n, 1024)

    x = jnp.concatenate([x_prompt.reshape(n_p, D_MODEL), x_sample.reshape(n_s, D_MODEL)], axis=0)
    outs = {k: [] for k in ('sbk_p', 'sbv_p', 'sbk_s', 'sbv_s', 'hg_p', 'hg_s', 'rw_p', 'rw_s', 'sh_p', 'sh_s',
                            'gl_p', 'gl_s')}
    for l in range(depth):
        P = _layer_params(l, norm_mix_pre, norm_mix_post, norm_ffn_pre, norm_ffn_post, w_in, w_out, w_up, w_down,
                          hg_norm, rw_mu, rw_w0, rw_w2, rw_a0, rw_a2, rw_g2, rw_k_k, rw_k_a, rw_r_k, rw_ln_w,
                          rw_ln_b, sb_bias, gla_a2, gla_a_b, gla_norm)
        proj_hg, proj_rw, proj_sb, proj_gla = _proj(x, P['norm_mix_pre'], P['w_in'], tm)

        rw_s3 = proj_rw[n_p:].reshape(bs, ts, RW_PROJ)
        prev_s = jnp.concatenate([state_rwkv_shift[l].astype(F32)[:, None], rw_s3[:, :-1]], axis=1)
        rw_src, rw_v, rw_g, rw_rkv, hg_src, gla_src = _prep(
            proj_hg, proj_rw, prev_s.reshape(n_s, RW_PROJ), proj_gla, e256, lb_all[l].reshape(1, GW), P,
            tm_prep, n_p, tp)

        def rec(planes, vsrc, vcol, sel, mode, s0_sample):
            dk = s0_sample.shape[1]
            y_p, s_p = _recurrence(planes, vsrc, vcol, sel, None, mode=mode, dk=dk, tok0=0, bsz=bp, tlen=tp)
            y_s, s_s = _recurrence(planes, vsrc, vcol, sel, s0_sample, mode=mode, dk=dk, tok0=n_p, bsz=bs, tlen=ts)
            return jnp.concatenate([y_p, y_s], axis=0), s_p, s_s

        y_hg, hgS_p, hgS_s = rec(hg_src, proj_hg, 2, sel_hg, 'gla', _state_in(state_hgrn[l]))
        rw_s0 = _state_in(jnp.swapaxes(state_rwkv[l], -1, -2))
        y_rw, rwS_p, rwS_s = rec(rw_src, rw_v, 0, sel_rw, 'rwkv', rw_s0)
        y_gla, glS_p, glS_s = rec(gla_src, proj_gla, 1, sel_gla, 'gla', _state_in(state_gla[l]))

        o_sb_p = _sb_prompt(proj_sb, P['sb_bias'], u_p, bsz=bp, tlen=tp, tq=tq)
        o_sb_s = _sb_sample(proj_sb, cache_k, cache_v, l, page_table, P['sb_bias'], uo_s,
                            tok0=n_p, bsz=bs, tnew=ts).reshape(n_s, GW)
        o_sb = jnp.concatenate([o_sb_p, o_sb_s], axis=0)

        x = _out_proj(x, y_hg, proj_hg, y_rw, rw_rkv, rw_g, o_sb, y_gla, proj_gla, e256, P, tm)
        x = _ffn(x, P['norm_ffn_pre'], P['norm_ffn_post'], P['w_up'], P['w_down'], tm_ffn, 1024)

        sb_p = proj_sb[:n_p].reshape(bp, tp, 3, HEADS, HEAD_DIM)
        sb_s = proj_sb[n_p:].reshape(bs, ts, 3, HEADS, HEAD_DIM)
        rw_p3 = proj_rw[:n_p].reshape(bp, tp, RW_PROJ)
        outs['sbk_p'].append(sb_p[:, :, 1]); outs['sbv_p'].append(sb_p[:, :, 2])
        outs['sbk_s'].append(sb_s[:, :, 1]); outs['sbv_s'].append(sb_s[:, :, 2])
        outs['hg_p'].append(_state_out(hgS_p)); outs['hg_s'].append(_state_out(hgS_s).astype(state_hgrn.dtype))
        outs['rw_p'].append(jnp.swapaxes(_state_out(rwS_p), -1, -2))
        outs['rw_s'].append(jnp.swapaxes(_state_out(rwS_s), -1, -2).astype(state_rwkv.dtype))
        outs['sh_p'].append(rw_p3[:, -1]); outs['sh_s'].append(rw_s3[:, -1].astype(state_rwkv_shift.dtype))
        outs['gl_p'].append(_state_out(glS_p)); outs['gl_s'].append(_state_out(glS_s).astype(state_gla.dtype))

    st = lambda k: jnp.stack(outs[k], 0)
    return (x[:n_p].reshape(bp, tp, D_MODEL), x[n_p:].reshape(bs, ts, D_MODEL),
            st('sbk_p'), st('sbv_p'), st('sbk_s'), st('sbv_s'), st('hg_p'), st('hg_s'),
            st('rw_p'), st('rw_s'), st('sh_p'), st('sh_s'), st('gl_p'), st('gl_s'))
```

```python
import functools

import jax
import jax.numpy as jnp
import numpy as np
from jax import lax
from jax.experimental import pallas as pl
from jax.experimental.pallas import tpu as pltpu

F32 = jnp.float32
BF16 = jnp.bfloat16

D_MODEL = 1024
HEADS = 4
HEAD_DIM = 64
GW = HEADS * HEAD_DIM
GLA_DK = 32
GLA_KW = HEADS * GLA_DK
GLA_RANK = 16
GLA_GATE_NORM = 16.0
RW_PROJ = 896
RW_GN_EPS = 64e-5
D_FF = 4096
RMS_EPS = 1e-6
PAGE = 128
LANES = 128
RW_PLANES = (16, 128)
HG_PLANES = (8, 128)
GLA_PLANES = (16, 32)
REC_TILE = 128
REC_UNROLL = 128
VMEM_LIMIT = 52 * 1024 * 1024

HG_W = 4 * GW
SB_W = 3 * GW
GLA_W = 896


def _cparams(sem):
    return pltpu.CompilerParams(dimension_semantics=sem, vmem_limit_bytes=VMEM_LIMIT)


def _split2(x):
    hi = x.astype(BF16)
    lo = (x - hi.astype(F32)).astype(BF16)
    return hi, lo


def _dot(a, b):
    return jnp.dot(a, b, preferred_element_type=F32)


def _dot_exact_rhs(x, e):
    hi, lo = _split2(x)
    return _dot(hi, e) + _dot(lo, e)


def _dot3(a, b):
    ah, al = _split2(a)
    bh, bl = _split2(b)
    return _dot(ah, bh) + (_dot(ah, bl) + _dot(al, bh))


def _sigmoid(x):
    return 1.0 / (1.0 + jnp.exp(-x))


def _softplus(x):
    return jnp.maximum(x, 0.0) + jnp.log1p(jnp.exp(-jnp.abs(x)))


def _rms_rows(x, g):
    return x * lax.rsqrt(jnp.mean(x * x, axis=-1, keepdims=True) + RMS_EPS) * g


def _proj_kernel(x_ref, g_ref, w_ref, hg_ref, rw_ref, sb_ref, gla_ref):
    h = _rms_rows(x_ref[...], g_ref[...]).astype(BF16)
    c0 = 0
    for ref, width in ((hg_ref, HG_W), (rw_ref, RW_PROJ), (sb_ref, SB_W), (gla_ref, GLA_W)):
        ref[...] = _dot(h, w_ref[:, c0:c0 + width])
        c0 += width


def _proj(x, g, w, tm):
    n = x.shape[0]
    wtot = HG_W + RW_PROJ + SB_W + GLA_W
    row = lambda i: (i, 0)
    fixed = lambda i: (0, 0)
    return pl.pallas_call(
        _proj_kernel,
        grid=(n // tm,),
        in_specs=[pl.BlockSpec((tm, D_MODEL), row), pl.BlockSpec((1, D_MODEL), fixed),
                  pl.BlockSpec((D_MODEL, wtot), fixed)],
        out_specs=[pl.BlockSpec((tm, HG_W), row), pl.BlockSpec((tm, RW_PROJ), row),
                   pl.BlockSpec((tm, SB_W), row), pl.BlockSpec((tm, GLA_W), row)],
        out_shape=[jax.ShapeDtypeStruct((n, HG_W), F32), jax.ShapeDtypeStruct((n, RW_PROJ), F32),
                   jax.ShapeDtypeStruct((n, SB_W), F32), jax.ShapeDtypeStruct((n, GLA_W), F32)],
        compiler_params=_cparams(("parallel",)),
        name="in_proj",
    )(x, g, w)


def _store_planes(ref, vecs):
    w = ref.shape[1]
    r = 0
    for x in vecs:
        for p in range(x.shape[1] // w):
            ref[r * REC_TILE:(r + 1) * REC_TILE, :] = x[:, p * w:(p + 1) * w]
            r += 1
    for rr in range(r, ref.shape[0] // REC_TILE):
        ref[rr * REC_TILE:(rr + 1) * REC_TILE, :] = jnp.zeros((REC_TILE, w), F32)


def _prep_kernel(hg_ref, rw_ref, tail_ref, prevs_ref, glaqk_ref, glaga_ref, e_ref, lb_ref, mu_ref, w0_ref, a0_ref,
                 kk_ref, ka_ref, rk_ref, w2_ref, a2_ref, g2_ref, ga2_ref, gab_ref,
                 rwsrc_ref, rwv_ref, rwg_ref, rwrkv_ref, hgsrc_ref, glasrc_ref, *, prompt_tiles, seq_tiles):
    e = e_ref[...]
    i = pl.program_id(0)
    hq = hg_ref[:, 0:GW]
    hf = hg_ref[:, GW:2 * GW]
    lb = lb_ref[...]
    f = lb + (1.0 - lb) * _sigmoid(hf)
    _store_planes(hgsrc_ref, (f, hq * _sigmoid(hq)))

    p = rw_ref[...]
    row0 = jnp.where(i % seq_tiles == 0, 0.0, tail_ref[7:8, :])
    first = lax.broadcasted_iota(jnp.int32, p.shape, 0) == 0
    prev = jnp.where(first, row0, pltpu.roll(p, 1, axis=0))
    prev = jnp.where(i >= prompt_tiles, prevs_ref[...], prev)
    xs = p + (prev - p) * mu_ref[...]
    r = xs[:, 0:GW]
    k = xs[:, GW:2 * GW]
    v = xs[:, 2 * GW:3 * GW]
    lora = xs[:, 3 * GW:RW_PROJ]
    w = -_softplus(-(w0_ref[...] + _dot3(jnp.tanh(lora), w2_ref[...]))) - 0.5
    decay = jnp.exp(-jnp.exp(w))
    a = _sigmoid(a0_ref[...] + _dot3(lora, a2_ref[...]))
    g = _dot3(_sigmoid(lora), g2_ref[...])
    kk = k * kk_ref[...]
    kk = kk / jnp.maximum(jnp.sqrt(_dot_exact_rhs(kk * kk, e)), 1e-12)
    kp = k * (1.0 + (a - 1.0) * ka_ref[...])
    rwv_ref[...] = v
    rwg_ref[...] = g
    rwrkv_ref[...] = _dot_exact_rhs(r * kp * rk_ref[...], e) * v
    _store_planes(rwsrc_ref, (kk, decay, kk * a, kp, r))

    gq = glaqk_ref[:, 0:GLA_KW]
    gk = glaqk_ref[:, GLA_KW:2 * GLA_KW]
    xg = _dot3(glaga_ref[...], ga2_ref[...]) + gab_ref[...]
    dec = jnp.exp(-_softplus(-xg) / GLA_GATE_NORM)
    _store_planes(glasrc_ref, (dec, gk, gq * (GLA_DK ** -0.5)))


def _prep(proj_hg, proj_rw, prev_s, proj_gla, e256, lb, P, tm, n_p, tp):
    n = proj_hg.shape[0]
    assert tp % tm == 0 and n_p % tm == 0 and prev_s.shape[0] % tm == 0 and tm % 8 == 0
    prompt_tiles = n_p // tm
    row = lambda i: (i, 0)
    fixed = lambda i: (0, 0)
    vec = lambda wd: pl.BlockSpec((1, wd), fixed)
    in_specs = [
        pl.BlockSpec((tm, 2 * GW), row),
        pl.BlockSpec((tm, RW_PROJ), row),
        pl.BlockSpec((8, RW_PROJ), lambda i: (jnp.maximum(i * (tm // 8) - 1, 0), 0)),
        pl.BlockSpec((tm, RW_PROJ), lambda i: (jnp.maximum(i - prompt_tiles, 0), 0)),
        pl.BlockSpec((tm, 2 * GLA_KW), row),
        pl.BlockSpec((tm, LANES), lambda i: (i, 6)),
        pl.BlockSpec((GW, GW), fixed),
        vec(GW), vec(RW_PROJ), vec(GW), vec(GW), vec(GW), vec(GW), vec(GW),
        pl.BlockSpec((LANES, GW), fixed), pl.BlockSpec((LANES, GW), fixed), pl.BlockSpec((LANES, GW), fixed),
        pl.BlockSpec((LANES, GLA_KW), fixed), vec(GLA_KW),
    ]
    assert tm == REC_TILE
    plane = lambda shape: (jax.ShapeDtypeStruct((n // tm, shape[0] * tm, shape[1]), F32),
                           pl.BlockSpec((None, shape[0] * tm, shape[1]), lambda i: (i, 0, 0)))
    flat = (jax.ShapeDtypeStruct((n, GW), F32), pl.BlockSpec((tm, GW), row))
    outs = [plane(RW_PLANES), flat, flat, flat, plane(HG_PLANES), plane(GLA_PLANES)]
    out_shapes = [o[0] for o in outs]
    out_specs = [o[1] for o in outs]
    return pl.pallas_call(
        functools.partial(_prep_kernel, prompt_tiles=prompt_tiles, seq_tiles=tp // tm),
        grid=(n // tm,), in_specs=in_specs, out_specs=out_specs, out_shape=out_shapes,
        compiler_params=_cparams(("parallel",)), name="mixer_prep",
    )(proj_hg, proj_rw, proj_rw, prev_s, proj_gla, proj_gla, e256, lb, P['rw_mu'], P['rw_w0'], P['rw_a0'],
      P['rw_k_k'], P['rw_k_a'], P['rw_r_k'], P['rw_w2p'], P['rw_a2p'], P['rw_g2p'], P['gla_a2p'], P['gla_a_b'])


def _rec_token(mode, dk, lt_ref, v_ref, sel_ref, y_ref, row, s):
    rows = lt_ref.shape[0] // REC_TILE
    planes = lt_ref[pl.ds(row, rows, stride=REC_TILE), :]
    groups = planes.shape[1] // dk
    x = planes if groups == 1 else jnp.concatenate([planes[:, o * dk:(o + 1) * dk] for o in range(groups)], axis=0)
    hi, lo = _split2(x)
    lhs = jnp.concatenate([hi, lo], axis=0)
    z = lax.dot_general(lhs, sel_ref[...], (((0,), (0,)), ((), ())), preferred_element_type=F32)
    v = v_ref[pl.ds(row, 1), :]
    if mode == 'rwkv':
        kk, w, kka, k, r = (z[:, i * GW:(i + 1) * GW] for i in range(5))
        sa = -jnp.sum(s * kk, axis=0, keepdims=True)
        s = s * w + kka * sa + k * v
        y = jnp.sum(s * r, axis=0, keepdims=True)
    else:
        if mode == 'hgrn':
            g, q = (z[:, i * GW:(i + 1) * GW] for i in range(2))
            k = 1.0 - g
        else:
            g, k, q = (z[:, i * GW:(i + 1) * GW] for i in range(3))
        s = s * g + k * v
        y = jnp.sum(s * q, axis=0, keepdims=True)
    y_ref[pl.ds(row, 1), :] = y
    return s


def _rec_long_kernel(lt_ref, v_ref, sel_ref, y_ref, s_ref, *, mode, dk):
    @pl.when(pl.program_id(1) == 0)
    def _():
        s_ref[...] = jnp.zeros_like(s_ref)

    def body(g, s):
        for u in range(REC_UNROLL):
            s = _rec_token(mode, dk, lt_ref, v_ref, sel_ref, y_ref, g * REC_UNROLL + u, s)
        return s

    s_ref[0] = lax.fori_loop(0, REC_TILE // REC_UNROLL, body, s_ref[0])


def _rec_short_kernel(lt_ref, v_ref, sel_ref, s0_ref, y_ref, s_ref, *, mode, dk, tlen):
    def body(q, _):
        for c in range(2):
            seq = 2 * q + c
            s = s0_ref[seq]
            for u in range(tlen):
                s = _rec_token(mode, dk, lt_ref, v_ref, sel_ref, y_ref, seq * tlen + u, s)
            s_ref[seq] = s
        return 0

    lax.fori_loop(0, REC_TILE // tlen // 2, body, 0)


def _recurrence(planes, vsrc, vcol, sel, s0, *, mode, dk, tok0, bsz, tlen):
    _, prow, pw = planes.shape
    off = tok0 // REC_TILE
    out_shape = [jax.ShapeDtypeStruct((bsz * tlen, GW), F32), jax.ShapeDtypeStruct((bsz, dk, GW), F32)]
    if s0 is None:
        nt = tlen // REC_TILE
        blk = lambda b, ti: off + b * nt + ti
        return pl.pallas_call(
            functools.partial(_rec_long_kernel, mode=mode, dk=dk), grid=(bsz, nt),
            in_specs=[pl.BlockSpec((None, prow, pw), lambda b, ti: (blk(b, ti), 0, 0)),
                      pl.BlockSpec((REC_TILE, GW), lambda b, ti: (blk(b, ti), vcol)),
                      pl.BlockSpec(sel.shape, lambda b, ti: (0, 0))],
            out_specs=[pl.BlockSpec((REC_TILE, GW), lambda b, ti: (b * nt + ti, 0)),
                       pl.BlockSpec((1, dk, GW), lambda b, ti: (b, 0, 0))],
            out_shape=out_shape, compiler_params=_cparams(("parallel", "arbitrary")), name="rec_long_" + mode,
        )(planes, vsrc, sel)
    nseq = REC_TILE // tlen
    return pl.pallas_call(
        functools.partial(_rec_short_kernel, mode=mode, dk=dk, tlen=tlen), grid=(bsz // nseq,),
        in_specs=[pl.BlockSpec((None, prow, pw), lambda i: (off + i, 0, 0)),
                  pl.BlockSpec((REC_TILE, GW), lambda i: (off + i, vcol)),
                  pl.BlockSpec(sel.shape, lambda i: (0, 0)),
                  pl.BlockSpec((nseq, dk, GW), lambda i: (i, 0, 0))],
        out_specs=[pl.BlockSpec((REC_TILE, GW), lambda i: (i, 0)),
                   pl.BlockSpec((nseq, dk, GW), lambda i: (i, 0, 0))],
        out_shape=out_shape, compiler_params=_cparams(("parallel",)), name="rec_short_" + mode,
    )(planes, vsrc, sel, s0)


def _log1m_sigmoid(z):
    return jnp.minimum(-z, 0.0) - jnp.log(1.0 + jnp.exp(-jnp.abs(z)))


def _sb_block(z, mask, u, carry, rowsum_lanes):
    l1m = _log1m_sigmoid(z)
    if mask is not None:
        l1m = jnp.where(mask, l1m, 0.0)
    hi, lo = _split2(l1m)
    between = carry + (_dot(hi, u) + _dot(lo, u))
    logw = z + l1m + between
    if mask is not None:
        logw = jnp.where(mask, logw, -jnp.inf)
    wgt = jnp.exp(logw)
    if rowsum_lanes is None:
        total = jnp.sum(l1m, axis=1, keepdims=True)
    else:
        total = _dot(hi, rowsum_lanes) + _dot(lo, rowsum_lanes)
    return wgt, carry + total


def _sbp_kernel(bias_ref, q_ref, k_ref, v_ref, u_ref, o_ref, *, tq):
    qi = pl.program_id(1)
    u = u_ref[...]
    nt_dims = (((1,), (1,)), ((), ()))
    rows = lax.broadcasted_iota(jnp.int32, (tq, tq), 0)
    cols = lax.broadcasted_iota(jnp.int32, (tq, tq), 1)
    lanes = [slice(h * HEAD_DIM, (h + 1) * HEAD_DIM) for h in range(HEADS)]
    qs = [(q_ref[:, lanes[h]] * (HEAD_DIM ** -0.5)).astype(BF16) for h in range(HEADS)]

    def block(j, state, mask):
        start = pl.multiple_of(j * tq, tq)
        out = []
        for h in range(HEADS):
            carry, acc = state[h]
            kb = k_ref[pl.ds(start, tq), lanes[h]].astype(BF16)
            vb = v_ref[pl.ds(start, tq), lanes[h]].astype(BF16)
            z = lax.dot_general(qs[h], kb, nt_dims, preferred_element_type=F32) + bias_ref[h]
            wgt, carry = _sb_block(z, mask, u, carry, None)
            out.append((carry, acc + _dot(wgt.astype(BF16), vb)))
        return tuple(out)

    zero = (jnp.zeros((tq, 1), F32), jnp.zeros((tq, HEAD_DIM), F32))
    state = block(qi, (zero,) * HEADS, cols < rows)
    state = lax.fori_loop(0, qi, lambda i, st: block(qi - 1 - i, st, None), state)
    for h in range(HEADS):
        o_ref[:, lanes[h]] = state[h][1]


def _sb_prompt(proj_sb, bias, u, *, bsz, tlen, tq):
    nq = tlen // tq
    grid_spec = pltpu.PrefetchScalarGridSpec(
        num_scalar_prefetch=1, grid=(bsz, nq),
        in_specs=[pl.BlockSpec((tq, GW), lambda b, qi, bias: (b * nq + qi, 0)),
                  pl.BlockSpec((tlen, GW), lambda b, qi, bias: (b, 1)),
                  pl.BlockSpec((tlen, GW), lambda b, qi, bias: (b, 2)),
                  pl.BlockSpec((tq, tq), lambda b, qi, bias: (0, 0))],
        out_specs=pl.BlockSpec((tq, GW), lambda b, qi, bias: (b * nq + qi, 0)))
    return pl.pallas_call(
        functools.partial(_sbp_kernel, tq=tq), grid_spec=grid_spec,
        out_shape=jax.ShapeDtypeStruct((bsz * tlen, GW), F32),
        compiler_params=_cparams(("parallel", "arbitrary")), name="sb_prompt",
    )(bias, proj_sb, proj_sb, proj_sb, u)


SBS_PAGES = 8


def _sbs_kernel(pt_ref, bias_ref, q_ref, kn_ref, vn_ref, *rest, tnew):
    kp_refs = rest[0:SBS_PAGES]
    vp_refs = rest[SBS_PAGES:2 * SBS_PAGES]
    uo_ref, o_ref, acc_ref, carry_ref, knew_ref, vnew_ref = rest[2 * SBS_PAGES:]
    s = pl.program_id(1)
    rows_n = HEADS * tnew
    uo = uo_ref[...]
    nt_dims = (((1,), (1,)), ((), ()))

    row_head = lax.broadcasted_iota(jnp.int32, (rows_n, GW), 0) // tnew
    lane_head = lax.broadcasted_iota(jnp.int32, (rows_n, GW), 1) // HEAD_DIM
    q = q_ref[...] * (HEAD_DIM ** -0.5)
    qbd = jnp.where(row_head == lane_head, jnp.concatenate([q] * HEADS, axis=0), 0.0).astype(BF16)
    rh = lax.broadcasted_iota(jnp.int32, (rows_n, PAGE), 0) // tnew
    bias = jnp.zeros((rows_n, PAGE), F32)
    for hh in range(HEADS):
        bias = jnp.where(rh == hh, bias_ref[hh], bias)

    def local(kb, mask):
        z = lax.dot_general(qbd, kb.astype(BF16), nt_dims, preferred_element_type=F32) + bias
        l1m = _log1m_sigmoid(z)
        if mask is not None:
            l1m = jnp.where(mask, l1m, 0.0)
        su = _dot_exact_rhs(l1m, uo)
        return z + l1m + su[:, 0:PAGE], su[:, PAGE:2 * PAGE]

    @pl.when(s == 0)
    def _():
        knew_ref[...] = jnp.zeros_like(knew_ref)
        vnew_ref[...] = jnp.zeros_like(vnew_ref)
        knew_ref[0:tnew, :] = kn_ref[...]
        vnew_ref[0:tnew, :] = vn_ref[...]
        trow = lax.broadcasted_iota(jnp.int32, (rows_n, PAGE), 0) % tnew
        col = lax.broadcasted_iota(jnp.int32, (rows_n, PAGE), 1)
        mask = col < trow
        logw, tot = local(knew_ref[...], mask)
        wgt = jnp.exp(jnp.where(mask, logw, -jnp.inf))
        acc_ref[...] = _dot(wgt.astype(BF16), vnew_ref[...].astype(BF16))
        carry_ref[...] = tot

    def page(ref):
        return jnp.concatenate([ref[pl.ds(h, PAGE, stride=HEADS), :] for h in range(HEADS)], axis=1)

    parts = [local(page(kp_refs[c]), None) for c in range(SBS_PAGES)]
    carry = carry_ref[...]
    acc = acc_ref[...]
    for c in range(SBS_PAGES):
        logw, tot = parts[c]
        acc += _dot(jnp.exp(logw + carry).astype(BF16), page(vp_refs[c]).astype(BF16))
        carry = carry + tot
    carry_ref[...] = carry
    acc_ref[...] = acc

    @pl.when(s == pl.num_programs(1) - 1)
    def _():
        acc = acc_ref[...]
        lh = lax.broadcasted_iota(jnp.int32, (tnew, GW), 1) // HEAD_DIM
        out = jnp.zeros((tnew, GW), F32)
        for hh in range(HEADS):
            out = jnp.where(lh == hh, acc[hh * tnew:(hh + 1) * tnew, :], out)
        o_ref[...] = out


def _sb_sample(proj_sb, cache_k, cache_v, layer, page_table, bias, uo, *, tok0, bsz, tnew):
    n = proj_sb.shape[0]
    npages = page_table.shape[1]
    nsteps = npages // SBS_PAGES
    p3 = proj_sb.reshape(n // tnew, tnew, SB_W)
    off = tok0 // tnew
    tok = lambda col: (lambda b, s, pt, bias: (off + b, 0, col))

    def page(c):
        return lambda b, s, pt, bias: (layer, pt[b, npages - 1 - (s * SBS_PAGES + c)], 0, 0)

    fixed = lambda b, s, pt, bias: (0, 0)
    in_specs = [pl.BlockSpec((None, tnew, GW), tok(0)), pl.BlockSpec((None, tnew, GW), tok(1)),
                pl.BlockSpec((None, tnew, GW), tok(2))]
    in_specs += 2 * [pl.BlockSpec((None, None, PAGE * HEADS, HEAD_DIM), page(c)) for c in range(SBS_PAGES)]
    in_specs += [pl.BlockSpec((PAGE, 2 * PAGE), fixed)]
    grid_spec = pltpu.PrefetchScalarGridSpec(
        num_scalar_prefetch=2, grid=(bsz, nsteps), in_specs=in_specs,
        out_specs=pl.BlockSpec((None, tnew, GW), lambda b, s, pt, bias: (b, 0, 0)),
        scratch_shapes=[pltpu.VMEM((HEADS * tnew, GW), F32), pltpu.VMEM((HEADS * tnew, PAGE), F32),
                        pltpu.VMEM((PAGE, GW), F32), pltpu.VMEM((PAGE, GW), F32)])
    return pl.pallas_call(
        functools.partial(_sbs_kernel, tnew=tnew), grid_spec=grid_spec,
        out_shape=jax.ShapeDtypeStruct((bsz, tnew, GW), F32),
        compiler_params=_cparams(("parallel", "arbitrary")), name="sb_sample",
    )(page_table, bias, p3, p3, p3, *([cache_k] * SBS_PAGES), *([cache_v] * SBS_PAGES), uo)


def _out_kernel(x_ref, yhg_ref, hgg_ref, yrw_ref, rwrkv_ref, rwg_ref, osb_ref, ygla_ref, glag_ref,
                e_ref, hgn_ref, lnw_ref, lnb_ref, glan_ref, post_ref, w_ref, o_ref):
    e = e_ref[...]
    inv = 1.0 / HEAD_DIM

    def head_rms(y, g):
        return y * lax.rsqrt(_dot_exact_rhs(y * y, e) * inv + RMS_EPS) * g

    def silu(t):
        return t * _sigmoid(t)

    o_hg = head_rms(yhg_ref[...], hgn_ref[...]) * silu(hgg_ref[...])
    y = yrw_ref[...]
    mean = _dot_exact_rhs(y, e) * inv
    yc = y - mean
    var = _dot_exact_rhs(yc * yc, e) * inv
    o_rw = (yc * lax.rsqrt(var + RW_GN_EPS) * lnw_ref[...] + lnb_ref[...] + rwrkv_ref[...]) * rwg_ref[...]
    o_gla = head_rms(ygla_ref[...], glan_ref[...]) * silu(glag_ref[...])
    acc = _dot(o_hg.astype(BF16), w_ref[0:GW, :])
    acc += _dot(o_rw.astype(BF16), w_ref[GW:2 * GW, :])
    acc += _dot(osb_ref[...].astype(BF16), w_ref[2 * GW:3 * GW, :])
    acc += _dot(o_gla.astype(BF16), w_ref[3 * GW:4 * GW, :])
    o_ref[...] = x_ref[...] + _rms_rows(acc, post_ref[...])


def _out_proj(x, y_hg, proj_hg, y_rw, rw_rkv, rw_g, o_sb, y_gla, proj_gla, e256, P, tm):
    n = x.shape[0]
    row = lambda i: (i, 0)
    fixed = lambda i: (0, 0)
    blk = pl.BlockSpec((tm, GW), row)
    vec = pl.BlockSpec((1, GW), fixed)
    in_specs = [pl.BlockSpec((tm, D_MODEL), row), blk, pl.BlockSpec((tm, GW), lambda i: (i, 3)),
                blk, blk, blk, blk, blk, pl.BlockSpec((tm, GW), lambda i: (i, 2)),
                pl.BlockSpec((GW, GW), fixed), vec, vec, vec, vec,
                pl.BlockSpec((1, D_MODEL), fixed), pl.BlockSpec((D_MODEL, D_MODEL), fixed)]
    return pl.pallas_call(
        _out_kernel, grid=(n // tm,), in_specs=in_specs, out_specs=pl.BlockSpec((tm, D_MODEL), row),
        out_shape=jax.ShapeDtypeStruct((n, D_MODEL), F32),
        compiler_params=_cparams(("parallel",)), name="out_proj",
    )(x, y_hg, proj_hg, y_rw, rw_rkv, rw_g, o_sb, y_gla, proj_gla, e256,
      P['hg_norm'], P['rw_ln_w'], P['rw_ln_b'], P['gla_norm'], P['norm_mix_post'], P['w_out'])


def _ffn_kernel(x_ref, gpre_ref, gpost_ref, wu_ref, wd_ref, o_ref, h_ref, acc_ref):
    f = pl.program_id(1)

    @pl.when(f == 0)
    def _():
        h_ref[...] = _rms_rows(x_ref[...], gpre_ref[...]).astype(BF16)
        acc_ref[...] = jnp.zeros_like(acc_ref)

    u = jnp.maximum(_dot(h_ref[...], wu_ref[...]), 0.0)
    acc_ref[...] += _dot((u * u).astype(BF16), wd_ref[...])

    @pl.when(f == pl.num_programs(1) - 1)
    def _():
        o_ref[...] = x_ref[...] + _rms_rows(acc_ref[...], gpost_ref[...])


def _ffn(x, gpre, gpost, wu, wd, tm, tf):
    n = x.shape[0]
    return pl.pallas_call(
        _ffn_kernel, grid=(n // tm, D_FF // tf),
        in_specs=[pl.BlockSpec((tm, D_MODEL), lambda i, f: (i, 0)),
                  pl.BlockSpec((1, D_MODEL), lambda i, f: (0, 0)), pl.BlockSpec((1, D_MODEL), lambda i, f: (0, 0)),
                  pl.BlockSpec((D_MODEL, tf), lambda i, f: (0, f)), pl.BlockSpec((tf, D_MODEL), lambda i, f: (f, 0))],
        out_specs=pl.BlockSpec((tm, D_MODEL), lambda i, f: (i, 0)),
        out_shape=jax.ShapeDtypeStruct((n, D_MODEL), F32),
        scratch_shapes=[pltpu.VMEM((tm, D_MODEL), BF16), pltpu.VMEM((tm, D_MODEL), F32)],
        compiler_params=_cparams(("parallel", "arbitrary")), name="ffn",
    )(x, gpre, gpost, wu, wd)


def _selector(nvec, plane_shape, dk):
    rows, width = plane_shape
    groups = width // dk
    k = np.arange(2 * groups * rows)
    grp, r = (k // rows) % groups, k % rows
    per_vec = HEADS * dk // width
    vec, head = r // per_vec, (r % per_vec) * groups + grp
    col = np.arange(nvec * GW)
    sel = (vec[:, None] == (col // GW)[None, :]) & (head[:, None] == ((col % GW) // HEAD_DIM)[None, :])
    return jnp.asarray(sel, BF16)


def _pad_rows(w, r0, rows):
    return jnp.zeros((rows, w.shape[1]), F32).at[r0:r0 + w.shape[0]].set(w.astype(F32))


def _layer_params(l, norm_mix_pre, norm_mix_post, norm_ffn_pre, norm_ffn_post, w_in, w_out, w_up, w_down,
                  hg_norm, rw_mu, rw_w0, rw_w2, rw_a0, rw_a2, rw_g2, rw_k_k, rw_k_a, rw_r_k, rw_ln_w, rw_ln_b,
                  sb_bias, gla_a2, gla_a_b, gla_norm):
    r = lambda a: a[l].astype(F32).reshape(1, -1)
    w = w_in[l]
    gla0 = HG_W + RW_PROJ + SB_W
    w_pad = jnp.concatenate(
        [w[:, :gla0 + 512], w[:, gla0 + 512 + GLA_RANK:], w[:, gla0 + 512:gla0 + 512 + GLA_RANK],
         jnp.zeros((D_MODEL, LANES - GLA_RANK), w.dtype)], axis=1).astype(BF16)
    return {
        'norm_mix_pre': r(norm_mix_pre), 'norm_mix_post': r(norm_mix_post),
        'norm_ffn_pre': r(norm_ffn_pre), 'norm_ffn_post': r(norm_ffn_post),
        'w_in': w_pad, 'w_out': w_out[l].astype(BF16), 'w_up': w_up[l].astype(BF16), 'w_down': w_down[l].astype(BF16),
        'hg_norm': jnp.tile(r(hg_norm), (1, HEADS)), 'gla_norm': jnp.tile(r(gla_norm), (1, HEADS)),
        'rw_mu': r(rw_mu), 'rw_w0': r(rw_w0), 'rw_a0': r(rw_a0), 'rw_k_k': r(rw_k_k), 'rw_k_a': r(rw_k_a),
        'rw_r_k': r(rw_r_k), 'rw_ln_w': r(rw_ln_w), 'rw_ln_b': r(rw_ln_b),
        'rw_w2p': _pad_rows(rw_w2[l], 0, LANES), 'rw_a2p': _pad_rows(rw_a2[l], 32, LANES),
        'rw_g2p': _pad_rows(rw_g2[l], 64, LANES),
        'gla_a2p': _pad_rows(gla_a2[l], 0, LANES), 'gla_a_b': r(gla_a_b),
        'sb_bias': sb_bias[l].astype(F32),
    }


def _tile(n, pref):
    t = pref
    while n % t:
        t //= 2
    return t


def _state_in(s):
    b, h, dk, dv = s.shape
    return jnp.transpose(s.astype(F32), (0, 2, 1, 3)).reshape(b, dk, h * dv)


def _state_out(s):
    b, dk, _ = s.shape
    return jnp.transpose(s.reshape(b, dk, HEADS, HEAD_DIM), (0, 2, 1, 3))


def kernel(x_prompt, x_sample, cache_sb_k, cache_sb_v, page_table, state_hgrn, state_rwkv, state_rwkv_shift,
           state_gla, norm_mix_pre, norm_mix_post, norm_ffn_pre, norm_ffn_post, w_in, w_out, w_up, w_down,
           hg_lb, hg_norm, rw_mu, rw_w0, rw_w2, rw_a0, rw_a2, rw_g2, rw_k_k, rw_k_a, rw_r_k, rw_ln_w, rw_ln_b,
           sb_bias, gla_a2, gla_a_b, gla_norm):
    depth = w_in.shape[0]
    bp, tp, _ = x_prompt.shape
    bs, ts, _ = x_sample.shape
    n_p, n_s = bp * tp, bs * ts
    n = n_p + n_s
    n_pool = cache_sb_k.shape[1]
    cache_k = cache_sb_k.reshape(depth, n_pool, PAGE * HEADS, HEAD_DIM)
    cache_v = cache_sb_v.reshape(depth, n_pool, PAGE * HEADS, HEAD_DIM)

    lb_all = jnp.cumsum(jax.nn.softmax(hg_lb.astype(F32), axis=0), axis=0)
    lb_all = lb_all - lb_all[0:1]
    e256 = jnp.asarray(np.arange(GW)[:, None] // HEAD_DIM == np.arange(GW)[None, :] // HEAD_DIM, BF16)
    sel_rw = _selector(5, RW_PLANES, HEAD_DIM)
    sel_hg = _selector(2, HG_PLANES, HEAD_DIM)
    sel_gla = _selector(3, GLA_PLANES, GLA_DK)
    tq = min(256, tp)
    u_p = jnp.asarray(np.arange(tq)[:, None] > np.arange(tq)[None, :], BF16)
    uo_s = jnp.asarray(np.concatenate([np.arange(PAGE)[:, None] > np.arange(PAGE)[None, :],
                                       np.ones((PAGE, PAGE), bool)], axis=1), BF16)
    tm = _tile(n, 512)
    tm_prep = REC_TILE
    tm_ffn = _tile(n, 1024)

    x = jnp.concatenate([x_prompt.reshape(n_p, D_MODEL), x_sample.reshape(n_s, D_MODEL)], axis=0)
    outs = {k: [] for k in ('sbk_p', 'sbv_p', 'sbk_s', 'sbv_s', 'hg_p', 'hg_s', 'rw_p', 'rw_s', 'sh_p', 'sh_s',
                            'gl_p', 'gl_s')}
    for l in range(depth):
        P = _layer_params(l, norm_mix_pre, norm_mix_post, norm_ffn_pre, norm_ffn_post, w_in, w_out, w_up, w_down,
                          hg_norm, rw_mu, rw_w0, rw_w2, rw_a0, rw_a2, rw_g2, rw_k_k, rw_k_a, rw_r_k, rw_ln_w,
                          rw_ln_b, sb_bias, gla_a2, gla_a_b, gla_norm)
        proj_hg, proj_rw, proj_sb, proj_gla = _proj(x, P['norm_mix_pre'], P['w_in'], tm)

        rw_s3 = proj_rw[n_p:].reshape(bs, ts, RW_PROJ)
        prev_s = jnp.concatenate([state_rwkv_shift[l].astype(F32)[:, None], rw_s3[:, :-1]], axis=1)
        rw_src, rw_v, rw_g, rw_rkv, hg_src, gla_src = _prep(
            proj_hg, proj_rw, prev_s.reshape(n_s, RW_PROJ), proj_gla, e256, lb_all[l].reshape(1, GW), P,
            tm_prep, n_p, tp)

        def rec(planes, vsrc, vcol, sel, mode, s0_sample):
            dk = s0_sample.shape[1]
            y_p, s_p = _recurrence(planes, vsrc, vcol, sel, None, mode=mode, dk=dk, tok0=0, bsz=bp, tlen=tp)
            y_s, s_s = _recurrence(planes, vsrc, vcol, sel, s0_sample, mode=mode, dk=dk, tok0=n_p, bsz=bs, tlen=ts)
            return jnp.concatenate([y_p, y_s], axis=0), s_p, s_s

        y_hg, hgS_p, hgS_s = rec(hg_src, proj_hg, 2, sel_hg, 'hgrn', _state_in(state_hgrn[l]))
        rw_s0 = _state_in(jnp.swapaxes(state_rwkv[l], -1, -2))
        y_rw, rwS_p, rwS_s = rec(rw_src, rw_v, 0, sel_rw, 'rwkv', rw_s0)
        y_gla, glS_p, glS_s = rec(gla_src, proj_gla, 1, sel_gla, 'gla', _state_in(state_gla[l]))

        o_sb_p = _sb_prompt(proj_sb, P['sb_bias'], u_p, bsz=bp, tlen=tp, tq=tq)
        o_sb_s = _sb_sample(proj_sb, cache_k, cache_v, l, page_table, P['sb_bias'], uo_s,
                            tok0=n_p, bsz=bs, tnew=ts).reshape(n_s, GW)
        o_sb = jnp.concatenate([o_sb_p, o_sb_s], axis=0)

        x = _out_proj(x, y_hg, proj_hg, y_rw, rw_rkv, rw_g, o_sb, y_gla, proj_gla, e256, P, tm)
        x = _ffn(x, P['norm_ffn_pre'], P['norm_ffn_post'], P['w_up'], P['w_down'], tm_ffn, 1024)

        sb_p = proj_sb[:n_p].reshape(bp, tp, 3, HEADS, HEAD_DIM)
        sb_s = proj_sb[n_p:].reshape(bs, ts, 3, HEADS, HEAD_DIM)
        rw_p3 = proj_rw[:n_p].reshape(bp, tp, RW_PROJ)
        outs['sbk_p'].append(sb_p[:, :, 1]); outs['sbv_p'].append(sb_p[:, :, 2])
        outs['sbk_s'].append(sb_s[:, :, 1]); outs['sbv_s'].append(sb_s[:, :, 2])
        outs['hg_p'].append(_state_out(hgS_p)); outs['hg_s'].append(_state_out(hgS_s).astype(state_hgrn.dtype))
        outs['rw_p'].append(jnp.swapaxes(_state_out(rwS_p), -1, -2))
        outs['rw_s'].append(jnp.swapaxes(_state_out(rwS_s), -1, -2).astype(state_rwkv.dtype))
        outs['sh_p'].append(rw_p3[:, -1]); outs['sh_s'].append(rw_s3[:, -1].astype(state_rwkv_shift.dtype))
        outs['gl_p'].append(_state_out(glS_p)); outs['gl_s'].append(_state_out(glS_s).astype(state_gla.dtype))

    st = lambda k: jnp.stack(outs[k], 0)
    return (x[:n_p].reshape(bp, tp, D_MODEL), x[n_p:].reshape(bs, ts, D_MODEL),
            st('sbk_p'), st('sbv_p'), st('sbk_s'), st('sbv_s'), st('hg_p'), st('hg_s'),
            st('rw_p'), st('rw_s'), st('sh_p'), st('sh_s'), st('gl_p'), st('gl_s'))
```

```python
import functools

import jax
import jax.numpy as jnp
import numpy as np
from jax import lax
from jax.experimental import pallas as pl
from jax.experimental.pallas import tpu as pltpu

F32 = jnp.float32
BF16 = jnp.bfloat16

D_MODEL = 1024
HEADS = 4
HEAD_DIM = 64
GW = HEADS * HEAD_DIM
GLA_DK = 32
GLA_KW = HEADS * GLA_DK
GLA_RANK = 16
GLA_GATE_NORM = 16.0
RW_PROJ = 896
RW_GN_EPS = 64e-5
D_FF = 4096
RMS_EPS = 1e-6
PAGE = 128
LANES = 128
RW_PLANES = (16, 128)
HG_PLANES = (8, 128)
GLA_PLANES = (16, 32)
REC_TILE = 128
REC_UNROLL = 128
VMEM_LIMIT = 52 * 1024 * 1024

HG_W = 4 * GW
SB_W = 3 * GW
GLA_W = 896


def _cparams(sem):
    return pltpu.CompilerParams(dimension_semantics=sem, vmem_limit_bytes=VMEM_LIMIT)


def _split2(x):
    hi = x.astype(BF16)
    lo = (x - hi.astype(F32)).astype(BF16)
    return hi, lo


def _dot(a, b):
    return jnp.dot(a, b, preferred_element_type=F32)


def _dot_exact_rhs(x, e):
    hi, lo = _split2(x)
    return _dot(hi, e) + _dot(lo, e)


def _dot3(a, b):
    ah, al = _split2(a)
    bh, bl = _split2(b)
    return _dot(ah, bh) + (_dot(ah, bl) + _dot(al, bh))


def _sigmoid(x):
    return 1.0 / (1.0 + jnp.exp(-x))


def _softplus(x):
    return jnp.maximum(x, 0.0) + jnp.log1p(jnp.exp(-jnp.abs(x)))


def _rms_rows(x, g):
    return x * lax.rsqrt(jnp.mean(x * x, axis=-1, keepdims=True) + RMS_EPS) * g


def _proj_kernel(x_ref, g_ref, w_ref, hg_ref, rw_ref, sb_ref, gla_ref):
    h = _rms_rows(x_ref[...], g_ref[...]).astype(BF16)
    c0 = 0
    for ref, width in ((hg_ref, HG_W), (rw_ref, RW_PROJ), (sb_ref, SB_W), (gla_ref, GLA_W)):
        ref[...] = _dot(h, w_ref[:, c0:c0 + width])
        c0 += width


def _proj(x, g, w, tm):
    n = x.shape[0]
    wtot = HG_W + RW_PROJ + SB_W + GLA_W
    row = lambda i: (i, 0)
    fixed = lambda i: (0, 0)
    return pl.pallas_call(
        _proj_kernel,
        grid=(n // tm,),
        in_specs=[pl.BlockSpec((tm, D_MODEL), row), pl.BlockSpec((1, D_MODEL), fixed),
                  pl.BlockSpec((D_MODEL, wtot), fixed)],
        out_specs=[pl.BlockSpec((tm, HG_W), row), pl.BlockSpec((tm, RW_PROJ), row),
                   pl.BlockSpec((tm, SB_W), row), pl.BlockSpec((tm, GLA_W), row)],
        out_shape=[jax.ShapeDtypeStruct((n, HG_W), F32), jax.ShapeDtypeStruct((n, RW_PROJ), F32),
                   jax.ShapeDtypeStruct((n, SB_W), F32), jax.ShapeDtypeStruct((n, GLA_W), F32)],
        compiler_params=_cparams(("parallel",)),
        name="in_proj",
    )(x, g, w)


def _store_planes(ref, vecs):
    w = ref.shape[1]
    r = 0
    for x in vecs:
        for p in range(x.shape[1] // w):
            ref[r * REC_TILE:(r + 1) * REC_TILE, :] = x[:, p * w:(p + 1) * w]
            r += 1
    for rr in range(r, ref.shape[0] // REC_TILE):
        ref[rr * REC_TILE:(rr + 1) * REC_TILE, :] = jnp.zeros((REC_TILE, w), F32)


def _prep_kernel(hg_ref, rw_ref, tail_ref, prevs_ref, glaqk_ref, glaga_ref, e_ref, lb_ref, mu_ref, w0_ref, a0_ref,
                 kk_ref, ka_ref, rk_ref, w2_ref, a2_ref, g2_ref, ga2_ref, gab_ref,
                 rwsrc_ref, rwv_ref, rwg_ref, rwrkv_ref, hgsrc_ref, glasrc_ref, *, prompt_tiles, seq_tiles):
    e = e_ref[...]
    i = pl.program_id(0)
    hq = hg_ref[:, 0:GW]
    hf = hg_ref[:, GW:2 * GW]
    lb = lb_ref[...]
    f = lb + (1.0 - lb) * _sigmoid(hf)
    _store_planes(hgsrc_ref, (f, hq * _sigmoid(hq)))

    p = rw_ref[...]
    row0 = jnp.where(i % seq_tiles == 0, 0.0, tail_ref[7:8, :])
    first = lax.broadcasted_iota(jnp.int32, p.shape, 0) == 0
    prev = jnp.where(first, row0, pltpu.roll(p, 1, axis=0))
    prev = jnp.where(i >= prompt_tiles, prevs_ref[...], prev)
    xs = p + (prev - p) * mu_ref[...]
    r = xs[:, 0:GW]
    k = xs[:, GW:2 * GW]
    v = xs[:, 2 * GW:3 * GW]
    lora = xs[:, 3 * GW:RW_PROJ]
    w = -_softplus(-(w0_ref[...] + _dot3(jnp.tanh(lora), w2_ref[...]))) - 0.5
    decay = jnp.exp(-jnp.exp(w))
    a = _sigmoid(a0_ref[...] + _dot3(lora, a2_ref[...]))
    g = _dot3(_sigmoid(lora), g2_ref[...])
    kk = k * kk_ref[...]
    kk = kk / jnp.maximum(jnp.sqrt(_dot_exact_rhs(kk * kk, e)), 1e-12)
    kp = k * (1.0 + (a - 1.0) * ka_ref[...])
    rwv_ref[...] = v
    rwg_ref[...] = g
    rwrkv_ref[...] = _dot_exact_rhs(r * kp * rk_ref[...], e) * v
    _store_planes(rwsrc_ref, (kk, decay, kk * a, kp, r))

    gq = glaqk_ref[:, 0:GLA_KW]
    gk = glaqk_ref[:, GLA_KW:2 * GLA_KW]
    xg = _dot3(glaga_ref[...], ga2_ref[...]) + gab_ref[...]
    dec = jnp.exp(-_softplus(-xg) / GLA_GATE_NORM)
    _store_planes(glasrc_ref, (dec, gk, gq * (GLA_DK ** -0.5)))


def _prep(proj_hg, proj_rw, prev_s, proj_gla, e256, lb, P, tm, n_p, tp):
    n = proj_hg.shape[0]
    assert tp % tm == 0 and n_p % tm == 0 and prev_s.shape[0] % tm == 0 and tm % 8 == 0
    prompt_tiles = n_p // tm
    row = lambda i: (i, 0)
    fixed = lambda i: (0, 0)
    vec = lambda wd: pl.BlockSpec((1, wd), fixed)
    in_specs = [
        pl.BlockSpec((tm, 2 * GW), row),
        pl.BlockSpec((tm, RW_PROJ), row),
        pl.BlockSpec((8, RW_PROJ), lambda i: (jnp.maximum(i * (tm // 8) - 1, 0), 0)),
        pl.BlockSpec((tm, RW_PROJ), lambda i: (jnp.maximum(i - prompt_tiles, 0), 0)),
        pl.BlockSpec((tm, 2 * GLA_KW), row),
        pl.BlockSpec((tm, LANES), lambda i: (i, 6)),
        pl.BlockSpec((GW, GW), fixed),
        vec(GW), vec(RW_PROJ), vec(GW), vec(GW), vec(GW), vec(GW), vec(GW),
        pl.BlockSpec((LANES, GW), fixed), pl.BlockSpec((LANES, GW), fixed), pl.BlockSpec((LANES, GW), fixed),
        pl.BlockSpec((LANES, GLA_KW), fixed), vec(GLA_KW),
    ]
    assert tm == REC_TILE
    plane = lambda shape: (jax.ShapeDtypeStruct((n // tm, shape[0] * tm, shape[1]), F32),
                           pl.BlockSpec((None, shape[0] * tm, shape[1]), lambda i: (i, 0, 0)))
    flat = (jax.ShapeDtypeStruct((n, GW), F32), pl.BlockSpec((tm, GW), row))
    outs = [plane(RW_PLANES), flat, flat, flat, plane(HG_PLANES), plane(GLA_PLANES)]
    out_shapes = [o[0] for o in outs]
    out_specs = [o[1] for o in outs]
    return pl.pallas_call(
        functools.partial(_prep_kernel, prompt_tiles=prompt_tiles, seq_tiles=tp // tm),
        grid=(n // tm,), in_specs=in_specs, out_specs=out_specs, out_shape=out_shapes,
        compiler_params=_cparams(("parallel",)), name="mixer_prep",
    )(proj_hg, proj_rw, proj_rw, prev_s, proj_gla, proj_gla, e256, lb, P['rw_mu'], P['rw_w0'], P['rw_a0'],
      P['rw_k_k'], P['rw_k_a'], P['rw_r_k'], P['rw_w2p'], P['rw_a2p'], P['rw_g2p'], P['gla_a2p'], P['gla_a_b'])


def _rec_token(mode, dk, lt_ref, v_ref, sel_ref, y_ref, row, s):
    rows = lt_ref.shape[0] // REC_TILE
    planes = lt_ref[pl.ds(row, rows, stride=REC_TILE), :]
    groups = planes.shape[1] // dk
    x = planes if groups == 1 else jnp.concatenate([planes[:, o * dk:(o + 1) * dk] for o in range(groups)], axis=0)
    hi, lo = _split2(x)
    lhs = jnp.concatenate([hi, lo], axis=0)
    z = lax.dot_general(lhs, sel_ref[...], (((0,), (0,)), ((), ())), preferred_element_type=F32)
    v = v_ref[pl.ds(row, 1), :]
    if mode == 'rwkv':
        kk, w, kka, k, r = (z[:, i * GW:(i + 1) * GW] for i in range(5))
        sa = -jnp.sum(s * kk, axis=0, keepdims=True)
        s = s * w + kka * sa + k * v
        y = jnp.sum(s * r, axis=0, keepdims=True)
    else:
        if mode == 'hgrn':
            g, q = (z[:, i * GW:(i + 1) * GW] for i in range(2))
            k = 1.0 - g
        else:
            g, k, q = (z[:, i * GW:(i + 1) * GW] for i in range(3))
        s = s * g + k * v
        y = jnp.sum(s * q, axis=0, keepdims=True)
    y_ref[pl.ds(row, 1), :] = y
    return s


def _rec_long_kernel(lt_ref, v_ref, sel_ref, y_ref, s_ref, *, mode, dk):
    @pl.when(pl.program_id(1) == 0)
    def _():
        s_ref[...] = jnp.zeros_like(s_ref)

    def body(g, s):
        for u in range(REC_UNROLL):
            s = _rec_token(mode, dk, lt_ref, v_ref, sel_ref, y_ref, g * REC_UNROLL + u, s)
        return s

    s_ref[0] = lax.fori_loop(0, REC_TILE // REC_UNROLL, body, s_ref[0])


def _rec_short_kernel(lt_ref, v_ref, sel_ref, s0_ref, y_ref, s_ref, *, mode, dk, tlen):
    def body(q, _):
        for c in range(2):
            seq = 2 * q + c
            s = s0_ref[seq]
            for u in range(tlen):
                s = _rec_token(mode, dk, lt_ref, v_ref, sel_ref, y_ref, seq * tlen + u, s)
            s_ref[seq] = s
        return 0

    lax.fori_loop(0, REC_TILE // tlen // 2, body, 0)


def _recurrence(planes, vsrc, vcol, sel, s0, *, mode, dk, tok0, bsz, tlen):
    _, prow, pw = planes.shape
    off = tok0 // REC_TILE
    out_shape = [jax.ShapeDtypeStruct((bsz * tlen, GW), F32), jax.ShapeDtypeStruct((bsz, dk, GW), F32)]
    if s0 is None:
        nt = tlen // REC_TILE
        blk = lambda b, ti: off + b * nt + ti
        return pl.pallas_call(
            functools.partial(_rec_long_kernel, mode=mode, dk=dk), grid=(bsz, nt),
            in_specs=[pl.BlockSpec((None, prow, pw), lambda b, ti: (blk(b, ti), 0, 0)),
                      pl.BlockSpec((REC_TILE, GW), lambda b, ti: (blk(b, ti), vcol)),
                      pl.BlockSpec(sel.shape, lambda b, ti: (0, 0))],
            out_specs=[pl.BlockSpec((REC_TILE, GW), lambda b, ti: (b * nt + ti, 0)),
                       pl.BlockSpec((1, dk, GW), lambda b, ti: (b, 0, 0))],
            out_shape=out_shape, compiler_params=_cparams(("parallel", "arbitrary")), name="rec_long_" + mode,
        )(planes, vsrc, sel)
    nseq = REC_TILE // tlen
    return pl.pallas_call(
        functools.partial(_rec_short_kernel, mode=mode, dk=dk, tlen=tlen), grid=(bsz // nseq,),
        in_specs=[pl.BlockSpec((None, prow, pw), lambda i: (off + i, 0, 0)),
                  pl.BlockSpec((REC_TILE, GW), lambda i: (off + i, vcol)),
                  pl.BlockSpec(sel.shape, lambda i: (0, 0)),
                  pl.BlockSpec((nseq, dk, GW), lambda i: (i, 0, 0))],
        out_specs=[pl.BlockSpec((REC_TILE, GW), lambda i: (i, 0)),
                   pl.BlockSpec((nseq, dk, GW), lambda i: (i, 0, 0))],
        out_shape=out_shape, compiler_params=_cparams(("parallel",)), name="rec_short_" + mode,
    )(planes, vsrc, sel, s0)


def _log1m_sigmoid(z):
    return jnp.minimum(-z, 0.0) - jnp.log(1.0 + jnp.exp(-jnp.abs(z)))


def _sb_block(z, mask, u, carry, rowsum_lanes):
    l1m = _log1m_sigmoid(z)
    if mask is not None:
        l1m = jnp.where(mask, l1m, 0.0)
    hi, lo = _split2(l1m)
    between = carry + (_dot(hi, u) + _dot(lo, u))
    logw = z + l1m + between
    if mask is not None:
        logw = jnp.where(mask, logw, -jnp.inf)
    wgt = jnp.exp(logw)
    if rowsum_lanes is None:
        total = jnp.sum(l1m, axis=1, keepdims=True)
    else:
        total = _dot(hi, rowsum_lanes) + _dot(lo, rowsum_lanes)
    return wgt, carry + total


def _sbp_kernel(bias_ref, q_ref, k_ref, v_ref, u_ref, o_ref, *, tq):
    qi = pl.program_id(1)
    u = u_ref[...]
    nt_dims = (((1,), (1,)), ((), ()))
    rows = lax.broadcasted_iota(jnp.int32, (tq, tq), 0)
    cols = lax.broadcasted_iota(jnp.int32, (tq, tq), 1)
    lanes = [slice(h * HEAD_DIM, (h + 1) * HEAD_DIM) for h in range(HEADS)]
    qs = [(q_ref[:, lanes[h]] * (HEAD_DIM ** -0.5)).astype(BF16) for h in range(HEADS)]

    def block(j, state, mask):
        start = pl.multiple_of(j * tq, tq)
        out = []
        for h in range(HEADS):
            carry, acc = state[h]
            kb = k_ref[pl.ds(start, tq), lanes[h]].astype(BF16)
            vb = v_ref[pl.ds(start, tq), lanes[h]].astype(BF16)
            z = lax.dot_general(qs[h], kb, nt_dims, preferred_element_type=F32) + bias_ref[h]
            wgt, carry = _sb_block(z, mask, u, carry, None)
            out.append((carry, acc + _dot(wgt.astype(BF16), vb)))
        return tuple(out)

    zero = (jnp.zeros((tq, 1), F32), jnp.zeros((tq, HEAD_DIM), F32))
    state = block(qi, (zero,) * HEADS, cols < rows)
    state = lax.fori_loop(0, qi, lambda i, st: block(qi - 1 - i, st, None), state)
    for h in range(HEADS):
        o_ref[:, lanes[h]] = state[h][1]


def _sb_prompt(proj_sb, bias, u, *, bsz, tlen, tq):
    nq = tlen // tq
    grid_spec = pltpu.PrefetchScalarGridSpec(
        num_scalar_prefetch=1, grid=(bsz, nq),
        in_specs=[pl.BlockSpec((tq, GW), lambda b, qi, bias: (b * nq + qi, 0)),
                  pl.BlockSpec((tlen, GW), lambda b, qi, bias: (b, 1)),
                  pl.BlockSpec((tlen, GW), lambda b, qi, bias: (b, 2)),
                  pl.BlockSpec((tq, tq), lambda b, qi, bias: (0, 0))],
        out_specs=pl.BlockSpec((tq, GW), lambda b, qi, bias: (b * nq + qi, 0)))
    return pl.pallas_call(
        functools.partial(_sbp_kernel, tq=tq), grid_spec=grid_spec,
        out_shape=jax.ShapeDtypeStruct((bsz * tlen, GW), F32),
        compiler_params=_cparams(("parallel", "arbitrary")), name="sb_prompt",
    )(bias, proj_sb, proj_sb, proj_sb, u)


SBS_PAGES = 8


def _sbs_kernel(pt_ref, bias_ref, q_ref, kn_ref, vn_ref, *rest, tnew):
    kp_refs = rest[0:SBS_PAGES]
    vp_refs = rest[SBS_PAGES:2 * SBS_PAGES]
    uo_ref, o_ref, acc_ref, carry_ref, knew_ref, vnew_ref = rest[2 * SBS_PAGES:]
    s = pl.program_id(1)
    rows_n = HEADS * tnew
    uo = uo_ref[...]
    nt_dims = (((1,), (1,)), ((), ()))

    row_head = lax.broadcasted_iota(jnp.int32, (rows_n, GW), 0) // tnew
    lane_head = lax.broadcasted_iota(jnp.int32, (rows_n, GW), 1) // HEAD_DIM
    q = q_ref[...] * (HEAD_DIM ** -0.5)
    qbd = jnp.where(row_head == lane_head, jnp.concatenate([q] * HEADS, axis=0), 0.0).astype(BF16)
    rh = lax.broadcasted_iota(jnp.int32, (rows_n, PAGE), 0) // tnew
    bias = jnp.zeros((rows_n, PAGE), F32)
    for hh in range(HEADS):
        bias = jnp.where(rh == hh, bias_ref[hh], bias)

    def local(kb, mask):
        z = lax.dot_general(qbd, kb.astype(BF16), nt_dims, preferred_element_type=F32) + bias
        l1m = _log1m_sigmoid(z)
        if mask is not None:
            l1m = jnp.where(mask, l1m, 0.0)
        su = _dot_exact_rhs(l1m, uo)
        return z + l1m + su[:, 0:PAGE], su[:, PAGE:2 * PAGE]

    @pl.when(s == 0)
    def _():
        knew_ref[...] = jnp.zeros_like(knew_ref)
        vnew_ref[...] = jnp.zeros_like(vnew_ref)
        knew_ref[0:tnew, :] = kn_ref[...]
        vnew_ref[0:tnew, :] = vn_ref[...]
        trow = lax.broadcasted_iota(jnp.int32, (rows_n, PAGE), 0) % tnew
        col = lax.broadcasted_iota(jnp.int32, (rows_n, PAGE), 1)
        mask = col < trow
        logw, tot = local(knew_ref[...], mask)
        wgt = jnp.exp(jnp.where(mask, logw, -jnp.inf))
        acc_ref[...] = _dot(wgt.astype(BF16), vnew_ref[...].astype(BF16))
        carry_ref[...] = tot

    parts = [local(kp_refs[c][...], None) for c in range(SBS_PAGES)]
    carry = carry_ref[...]
    acc = acc_ref[...]
    for c in range(SBS_PAGES):
        logw, tot = parts[c]
        acc += _dot(jnp.exp(logw + carry).astype(BF16), vp_refs[c][...].astype(BF16))
        carry = carry + tot
    carry_ref[...] = carry
    acc_ref[...] = acc

    @pl.when(s == pl.num_programs(1) - 1)
    def _():
        acc = acc_ref[...]
        lh = lax.broadcasted_iota(jnp.int32, (tnew, GW), 1) // HEAD_DIM
        out = jnp.zeros((tnew, GW), F32)
        for hh in range(HEADS):
            out = jnp.where(lh == hh, acc[hh * tnew:(hh + 1) * tnew, :], out)
        o_ref[...] = out


def _sb_sample(proj_sb, cache_k, cache_v, layer, page_table, bias, uo, *, tok0, bsz, tnew):
    n = proj_sb.shape[0]
    npages = page_table.shape[1]
    nsteps = npages // SBS_PAGES
    p3 = proj_sb.reshape(n // tnew, tnew, SB_W)
    off = tok0 // tnew
    tok = lambda col: (lambda b, s, pt, bias: (off + b, 0, col))

    def page(c):
        return lambda b, s, pt, bias: (layer, pt[b, npages - 1 - (s * SBS_PAGES + c)], 0, 0)

    fixed = lambda b, s, pt, bias: (0, 0)
    in_specs = [pl.BlockSpec((None, tnew, GW), tok(0)), pl.BlockSpec((None, tnew, GW), tok(1)),
                pl.BlockSpec((None, tnew, GW), tok(2))]
    in_specs += 2 * [pl.BlockSpec((None, None, PAGE, GW), page(c)) for c in range(SBS_PAGES)]
    in_specs += [pl.BlockSpec((PAGE, 2 * PAGE), fixed)]
    grid_spec = pltpu.PrefetchScalarGridSpec(
        num_scalar_prefetch=2, grid=(bsz, nsteps), in_specs=in_specs,
        out_specs=pl.BlockSpec((None, tnew, GW), lambda b, s, pt, bias: (b, 0, 0)),
        scratch_shapes=[pltpu.VMEM((HEADS * tnew, GW), F32), pltpu.VMEM((HEADS * tnew, PAGE), F32),
                        pltpu.VMEM((PAGE, GW), F32), pltpu.VMEM((PAGE, GW), F32)])
    return pl.pallas_call(
        functools.partial(_sbs_kernel, tnew=tnew), grid_spec=grid_spec,
        out_shape=jax.ShapeDtypeStruct((bsz, tnew, GW), F32),
        compiler_params=_cparams(("parallel", "arbitrary")), name="sb_sample",
    )(page_table, bias, p3, p3, p3, *([cache_k] * SBS_PAGES), *([cache_v] * SBS_PAGES), uo)


def _out_kernel(x_ref, yhg_ref, hgg_ref, yrw_ref, rwrkv_ref, rwg_ref, osb_ref, ygla_ref, glag_ref,
                e_ref, hgn_ref, lnw_ref, lnb_ref, glan_ref, post_ref, w_ref, o_ref):
    e = e_ref[...]
    inv = 1.0 / HEAD_DIM

    def head_rms(y, g):
        return y * lax.rsqrt(_dot_exact_rhs(y * y, e) * inv + RMS_EPS) * g

    def silu(t):
        return t * _sigmoid(t)

    o_hg = head_rms(yhg_ref[...], hgn_ref[...]) * silu(hgg_ref[...])
    y = yrw_ref[...]
    mean = _dot_exact_rhs(y, e) * inv
    yc = y - mean
    var = _dot_exact_rhs(yc * yc, e) * inv
    o_rw = (yc * lax.rsqrt(var + RW_GN_EPS) * lnw_ref[...] + lnb_ref[...] + rwrkv_ref[...]) * rwg_ref[...]
    o_gla = head_rms(ygla_ref[...], glan_ref[...]) * silu(glag_ref[...])
    acc = _dot(o_hg.astype(BF16), w_ref[0:GW, :])
    acc += _dot(o_rw.astype(BF16), w_ref[GW:2 * GW, :])
    acc += _dot(osb_ref[...].astype(BF16), w_ref[2 * GW:3 * GW, :])
    acc += _dot(o_gla.astype(BF16), w_ref[3 * GW:4 * GW, :])
    o_ref[...] = x_ref[...] + _rms_rows(acc, post_ref[...])


def _out_proj(x, y_hg, proj_hg, y_rw, rw_rkv, rw_g, o_sb, y_gla, proj_gla, e256, P, tm):
    n = x.shape[0]
    row = lambda i: (i, 0)
    fixed = lambda i: (0, 0)
    blk = pl.BlockSpec((tm, GW), row)
    vec = pl.BlockSpec((1, GW), fixed)
    in_specs = [pl.BlockSpec((tm, D_MODEL), row), blk, pl.BlockSpec((tm, GW), lambda i: (i, 3)),
                blk, blk, blk, blk, blk, pl.BlockSpec((tm, GW), lambda i: (i, 2)),
                pl.BlockSpec((GW, GW), fixed), vec, vec, vec, vec,
                pl.BlockSpec((1, D_MODEL), fixed), pl.BlockSpec((D_MODEL, D_MODEL), fixed)]
    return pl.pallas_call(
        _out_kernel, grid=(n // tm,), in_specs=in_specs, out_specs=pl.BlockSpec((tm, D_MODEL), row),
        out_shape=jax.ShapeDtypeStruct((n, D_MODEL), F32),
        compiler_params=_cparams(("parallel",)), name="out_proj",
    )(x, y_hg, proj_hg, y_rw, rw_rkv, rw_g, o_sb, y_gla, proj_gla, e256,
      P['hg_norm'], P['rw_ln_w'], P['rw_ln_b'], P['gla_norm'], P['norm_mix_post'], P['w_out'])


def _ffn_kernel(x_ref, gpre_ref, gpost_ref, wu_ref, wd_ref, o_ref, h_ref, acc_ref):
    f = pl.program_id(1)

    @pl.when(f == 0)
    def _():
        h_ref[...] = _rms_rows(x_ref[...], gpre_ref[...]).astype(BF16)
        acc_ref[...] = jnp.zeros_like(acc_ref)

    u = jnp.maximum(_dot(h_ref[...], wu_ref[...]), 0.0)
    acc_ref[...] += _dot((u * u).astype(BF16), wd_ref[...])

    @pl.when(f == pl.num_programs(1) - 1)
    def _():
        o_ref[...] = x_ref[...] + _rms_rows(acc_ref[...], gpost_ref[...])


def _ffn(x, gpre, gpost, wu, wd, tm, tf):
    n = x.shape[0]
    return pl.pallas_call(
        _ffn_kernel, grid=(n // tm, D_FF // tf),
        in_specs=[pl.BlockSpec((tm, D_MODEL), lambda i, f: (i, 0)),
                  pl.BlockSpec((1, D_MODEL), lambda i, f: (0, 0)), pl.BlockSpec((1, D_MODEL), lambda i, f: (0, 0)),
                  pl.BlockSpec((D_MODEL, tf), lambda i, f: (0, f)), pl.BlockSpec((tf, D_MODEL), lambda i, f: (f, 0))],
        out_specs=pl.BlockSpec((tm, D_MODEL), lambda i, f: (i, 0)),
        out_shape=jax.ShapeDtypeStruct((n, D_MODEL), F32),
        scratch_shapes=[pltpu.VMEM((tm, D_MODEL), BF16), pltpu.VMEM((tm, D_MODEL), F32)],
        compiler_params=_cparams(("parallel", "arbitrary")), name="ffn",
    )(x, gpre, gpost, wu, wd)


def _selector(nvec, plane_shape, dk):
    rows, width = plane_shape
    groups = width // dk
    k = np.arange(2 * groups * rows)
    grp, r = (k // rows) % groups, k % rows
    per_vec = HEADS * dk // width
    vec, head = r // per_vec, (r % per_vec) * groups + grp
    col = np.arange(nvec * GW)
    sel = (vec[:, None] == (col // GW)[None, :]) & (head[:, None] == ((col % GW) // HEAD_DIM)[None, :])
    return jnp.asarray(sel, BF16)


def _pad_rows(w, r0, rows):
    return jnp.zeros((rows, w.shape[1]), F32).at[r0:r0 + w.shape[0]].set(w.astype(F32))


def _layer_params(l, norm_mix_pre, norm_mix_post, norm_ffn_pre, norm_ffn_post, w_in, w_out, w_up, w_down,
                  hg_norm, rw_mu, rw_w0, rw_w2, rw_a0, rw_a2, rw_g2, rw_k_k, rw_k_a, rw_r_k, rw_ln_w, rw_ln_b,
                  sb_bias, gla_a2, gla_a_b, gla_norm):
    r = lambda a: a[l].astype(F32).reshape(1, -1)
    w = w_in[l]
    gla0 = HG_W + RW_PROJ + SB_W
    w_pad = jnp.concatenate(
        [w[:, :gla0 + 512], w[:, gla0 + 512 + GLA_RANK:], w[:, gla0 + 512:gla0 + 512 + GLA_RANK],
         jnp.zeros((D_MODEL, LANES - GLA_RANK), w.dtype)], axis=1).astype(BF16)
    return {
        'norm_mix_pre': r(norm_mix_pre), 'norm_mix_post': r(norm_mix_post),
        'norm_ffn_pre': r(norm_ffn_pre), 'norm_ffn_post': r(norm_ffn_post),
        'w_in': w_pad, 'w_out': w_out[l].astype(BF16), 'w_up': w_up[l].astype(BF16), 'w_down': w_down[l].astype(BF16),
        'hg_norm': jnp.tile(r(hg_norm), (1, HEADS)), 'gla_norm': jnp.tile(r(gla_norm), (1, HEADS)),
        'rw_mu': r(rw_mu), 'rw_w0': r(rw_w0), 'rw_a0': r(rw_a0), 'rw_k_k': r(rw_k_k), 'rw_k_a': r(rw_k_a),
        'rw_r_k': r(rw_r_k), 'rw_ln_w': r(rw_ln_w), 'rw_ln_b': r(rw_ln_b),
        'rw_w2p': _pad_rows(rw_w2[l], 0, LANES), 'rw_a2p': _pad_rows(rw_a2[l], 32, LANES),
        'rw_g2p': _pad_rows(rw_g2[l], 64, LANES),
        'gla_a2p': _pad_rows(gla_a2[l], 0, LANES), 'gla_a_b': r(gla_a_b),
        'sb_bias': sb_bias[l].astype(F32),
    }


def _tile(n, pref):
    t = pref
    while n % t:
        t //= 2
    return t


def _state_in(s):
    b, h, dk, dv = s.shape
    return jnp.transpose(s.astype(F32), (0, 2, 1, 3)).reshape(b, dk, h * dv)


def _state_out(s):
    b, dk, _ = s.shape
    return jnp.transpose(s.reshape(b, dk, HEADS, HEAD_DIM), (0, 2, 1, 3))


def kernel(x_prompt, x_sample, cache_sb_k, cache_sb_v, page_table, state_hgrn, state_rwkv, state_rwkv_shift,
           state_gla, norm_mix_pre, norm_mix_post, norm_ffn_pre, norm_ffn_post, w_in, w_out, w_up, w_down,
           hg_lb, hg_norm, rw_mu, rw_w0, rw_w2, rw_a0, rw_a2, rw_g2, rw_k_k, rw_k_a, rw_r_k, rw_ln_w, rw_ln_b,
           sb_bias, gla_a2, gla_a_b, gla_norm):
    depth = w_in.shape[0]
    bp, tp, _ = x_prompt.shape
    bs, ts, _ = x_sample.shape
    n_p, n_s = bp * tp, bs * ts
    n = n_p + n_s
    n_pool = cache_sb_k.shape[1]
    cache_k = cache_sb_k.reshape(depth, n_pool, PAGE, GW)
    cache_v = cache_sb_v.reshape(depth, n_pool, PAGE, GW)

    lb_all = jnp.cumsum(jax.nn.softmax(hg_lb.astype(F32), axis=0), axis=0)
    lb_all = lb_all - lb_all[0:1]
    e256 = jnp.asarray(np.arange(GW)[:, None] // HEAD_DIM == np.arange(GW)[None, :] // HEAD_DIM, BF16)
    sel_rw = _selector(5, RW_PLANES, HEAD_DIM)
    sel_hg = _selector(2, HG_PLANES, HEAD_DIM)
    sel_gla = _selector(3, GLA_PLANES, GLA_DK)
    tq = min(256, tp)
    u_p = jnp.asarray(np.arange(tq)[:, None] > np.arange(tq)[None, :], BF16)
    uo_s = jnp.asarray(np.concatenate([np.arange(PAGE)[:, None] > np.arange(PAGE)[None, :],
                                       np.ones((PAGE, PAGE), bool)], axis=1), BF16)
    tm = _tile(n, 512)
    tm_prep = REC_TILE
    tm_ffn = _tile(n, 1024)

    x = jnp.concatenate([x_prompt.reshape(n_p, D_MODEL), x_sample.reshape(n_s, D_MODEL)], axis=0)
    outs = {k: [] for k in ('sbk_p', 'sbv_p', 'sbk_s', 'sbv_s', 'hg_p', 'hg_s', 'rw_p', 'rw_s', 'sh_p', 'sh_s',
                            'gl_p', 'gl_s')}
    for l in range(depth):
        P = _layer_params(l, norm_mix_pre, norm_mix_post, norm_ffn_pre, norm_ffn_post, w_in, w_out, w_up, w_down,
                          hg_norm, rw_mu, rw_w0, rw_w2, rw_a0, rw_a2, rw_g2, rw_k_k, rw_k_a, rw_r_k, rw_ln_w,
                          rw_ln_b, sb_bias, gla_a2, gla_a_b, gla_norm)
        proj_hg, proj_rw, proj_sb, proj_gla = _proj(x, P['norm_mix_pre'], P['w_in'], tm)

        rw_s3 = proj_rw[n_p:].reshape(bs, ts, RW_PROJ)
        prev_s = jnp.concatenate([state_rwkv_shift[l].astype(F32)[:, None], rw_s3[:, :-1]], axis=1)
        rw_src, rw_v, rw_g, rw_rkv, hg_src, gla_src = _prep(
            proj_hg, proj_rw, prev_s.reshape(n_s, RW_PROJ), proj_gla, e256, lb_all[l].reshape(1, GW), P,
            tm_prep, n_p, tp)

        def rec(planes, vsrc, vcol, sel, mode, s0_sample):
            dk = s0_sample.shape[1]
            y_p, s_p = _recurrence(planes, vsrc, vcol, sel, None, mode=mode, dk=dk, tok0=0, bsz=bp, tlen=tp)
            y_s, s_s = _recurrence(planes, vsrc, vcol, sel, s0_sample, mode=mode, dk=dk, tok0=n_p, bsz=bs, tlen=ts)
            return jnp.concatenate([y_p, y_s], axis=0), s_p, s_s

        y_hg, hgS_p, hgS_s = rec(hg_src, proj_hg, 2, sel_hg, 'hgrn', _state_in(state_hgrn[l]))
        rw_s0 = _state_in(jnp.swapaxes(state_rwkv[l], -1, -2))
        y_rw, rwS_p, rwS_s = rec(rw_src, rw_v, 0, sel_rw, 'rwkv', rw_s0)
        y_gla, glS_p, glS_s = rec(gla_src, proj_gla, 1, sel_gla, 'gla', _state_in(state_gla[l]))

        o_sb_p = _sb_prompt(proj_sb, P['sb_bias'], u_p, bsz=bp, tlen=tp, tq=tq)
        o_sb_s = _sb_sample(proj_sb, cache_k, cache_v, l, page_table, P['sb_bias'], uo_s,
                            tok0=n_p, bsz=bs, tnew=ts).reshape(n_s, GW)
        o_sb = jnp.concatenate([o_sb_p, o_sb_s], axis=0)

        x = _out_proj(x, y_hg, proj_hg, y_rw, rw_rkv, rw_g, o_sb, y_gla, proj_gla, e256, P, tm)
        x = _ffn(x, P['norm_ffn_pre'], P['norm_ffn_post'], P['w_up'], P['w_down'], tm_ffn, 1024)

        sb_p = proj_sb[:n_p].reshape(bp, tp, 3, HEADS, HEAD_DIM)
        sb_s = proj_sb[n_p:].reshape(bs, ts, 3, HEADS, HEAD_DIM)
        rw_p3 = proj_rw[:n_p].reshape(bp, tp, RW_PROJ)
        outs['sbk_p'].append(sb_p[:, :, 1]); outs['sbv_p'].append(sb_p[:, :, 2])
        outs['sbk_s'].append(sb_s[:, :, 1]); outs['sbv_s'].append(sb_s[:, :, 2])
        outs['hg_p'].append(_state_out(hgS_p)); outs['hg_s'].append(_state_out(hgS_s).astype(state_hgrn.dtype))
        outs['rw_p'].append(jnp.swapaxes(_state_out(rwS_p), -1, -2))
        outs['rw_s'].append(jnp.swapaxes(_state_out(rwS_s), -1, -2).astype(state_rwkv.dtype))
        outs['sh_p'].append(rw_p3[:, -1]); outs['sh_s'].append(rw_s3[:, -1].astype(state_rwkv_shift.dtype))
        outs['gl_p'].append(_state_out(glS_p)); outs['gl_s'].append(_state_out(glS_s).astype(state_gla.dtype))

    st = lambda k: jnp.stack(outs[k], 0)
    return (x[:n_p].reshape(bp, tp, D_MODEL), x[n_p:].reshape(bs, ts, D_MODEL),
            st('sbk_p'), st('sbv_p'), st('sbk_s'), st('sbv_s'), st('hg_p'), st('hg_s'),
            st('rw_p'), st('rw_s'), st('sh_p'), st('sh_s'), st('gl_p'), st('gl_s'))
```
